```python
import math
import jax, jax.numpy as jnp
from jax import lax
import numpy as np

D_MODEL = 1024
BATCH = 8
SEQ = 4096
DEPTH = 1

HEAD_DIM = 64
SB_HEADS = 8
SB_WIDTH = SB_HEADS * HEAD_DIM
DIFF_HEADS = 4
DIFF_QK_DIM = 64
DIFF_V_DIM = 2 * DIFF_QK_DIM
DIFF_QK_WIDTH = DIFF_HEADS * 2 * DIFF_QK_DIM
DIFF_V_WIDTH = DIFF_HEADS * DIFF_V_DIM
N_BRANCHES = 2
IN_SPLITS = (SB_WIDTH, SB_WIDTH, SB_WIDTH, DIFF_QK_WIDTH, DIFF_QK_WIDTH, DIFF_V_WIDTH, D_MODEL, D_MODEL)
IN_WIDTH = 3 * SB_WIDTH + 2 * DIFF_QK_WIDTH + DIFF_V_WIDTH + N_BRANCHES * D_MODEL
Q_BLOCK = 128
ROPE_THETA = 10000.0
N_GROUPS = 4
EXPERTS_PER_GROUP = 8
N_EXPERTS = N_GROUPS * EXPERTS_PER_GROUP
TOP_K = 2
D_EXPERT = 256
EXPERT_BLOCK = 128
EPS = 1e-6

kernel_name = "hybrid_stickbreak_diffattn_hiermoe"


def rmsnorm(x, g):
    xf = x.astype(jnp.float32)
    y = xf * lax.rsqrt(jnp.mean(xf * xf, axis=-1, keepdims=True) + EPS)
    return (y * g.astype(jnp.float32)).astype(x.dtype)


def rope_tables(seq_len, dim):
    pos = jnp.arange(seq_len, dtype=jnp.float32)
    inv_freq = 1.0 / (ROPE_THETA ** (jnp.arange(0, dim, 2, dtype=jnp.float32) / dim))
    freqs = pos[:, None] * inv_freq[None, :]
    emb = jnp.concatenate([freqs, freqs], axis=-1)
    return jnp.cos(emb), jnp.sin(emb)


def apply_rope(x, cos, sin):
    xf = x.astype(jnp.float32)
    half = xf.shape[-1] // 2
    rot = jnp.concatenate([-xf[..., half:], xf[..., :half]], axis=-1)
    return (xf * cos + rot * sin).astype(x.dtype)


def split_heads(t, n_heads, d):
    b, s, _ = t.shape
    return t.reshape(b, s, n_heads, d).transpose(0, 2, 1, 3)


def merge_heads(t):
    b, h, s, d = t.shape
    return t.transpose(0, 2, 1, 3).reshape(b, s, h * d)


def stick_breaking_attention(q, k, v):
    seq_len = q.shape[2]
    scale = HEAD_DIM ** -0.5
    outs = []
    for blk in range(seq_len // Q_BLOCK):
        q0 = blk * Q_BLOCK
        kv_len = q0 + Q_BLOCK
        qb = q[:, :, q0:kv_len]
        kb = k[:, :, :kv_len]
        vb = v[:, :, :kv_len]
        z = jnp.einsum('bhqd,bhkd->bhqk', qb, kb, preferred_element_type=jnp.float32) * scale
        qpos = q0 + jnp.arange(Q_BLOCK)[:, None]
        kpos = jnp.arange(kv_len)[None, :]
        strict = kpos < qpos
        log_1m_beta = jnp.where(strict, jax.nn.log_sigmoid(-z), 0.0)
        after = lax.cumsum(log_1m_beta, axis=3, reverse=True) - log_1m_beta
        a = jnp.where(strict, jnp.exp(jax.nn.log_sigmoid(z) + after), 0.0)
        outs.append(jnp.einsum('bhqk,bhkd->bhqd', a.astype(v.dtype), vb))
    return jnp.concatenate(outs, axis=2)


def differential_attention(q1, q2, k1, k2, v, lam):
    seq_len = q1.shape[2]
    scale = DIFF_QK_DIM ** -0.5
    outs = []
    for blk in range(seq_len // Q_BLOCK):
        q0 = blk * Q_BLOCK
        kv_len = q0 + Q_BLOCK
        qpos = q0 + jnp.arange(Q_BLOCK)[:, None]
        kpos = jnp.arange(kv_len)[None, :]
        causal = kpos <= qpos

        def probs(qx, kx):
            s = jnp.einsum('bhqd,bhkd->bhqk', qx[:, :, q0:kv_len], kx[:, :, :kv_len],
                           preferred_element_type=jnp.float32) * scale
            return jax.nn.softmax(jnp.where(causal, s, -jnp.inf), axis=-1)

        p = probs(q1, k1) - lam.astype(jnp.float32) * probs(q2, k2)
        outs.append(jnp.einsum('bhqk,bhkd->bhqd', p.astype(v.dtype), v[:, :, :kv_len]))
    return jnp.concatenate(outs, axis=2)


def hierarchical_moe(h, w_rg, b_rg, w_re, b_re, w_gate, w_up, w_down):
    b, s, d = h.shape
    n_tok = b * s
    t = h.reshape(n_tok, d)
    g_logits = (t @ w_rg + b_rg).astype(jnp.float32)
    grp = jnp.argmax(g_logits, axis=-1)
    p_grp = jnp.take_along_axis(jax.nn.softmax(g_logits, axis=-1), grp[:, None], axis=-1)
    e_logits = (t @ w_re + b_re).astype(jnp.float32).reshape(n_tok, N_GROUPS, EXPERTS_PER_GROUP)
    e_in = jnp.take_along_axis(e_logits, grp[:, None, None], axis=1)[:, 0]
    top_w, top_i = lax.top_k(jax.nn.softmax(e_in, axis=-1), TOP_K)
    top_w = top_w / jnp.sum(top_w, axis=-1, keepdims=True) * p_grp
    expert_id = grp[:, None] * EXPERTS_PER_GROUP + top_i

    n_assign = n_tok * TOP_K
    flat_e = expert_id.reshape(n_assign)
    flat_w = top_w.reshape(n_assign)
    flat_tok = jnp.arange(n_assign) // TOP_K
    order = jnp.argsort(flat_e)
    s_e, s_tok, s_w = flat_e[order], flat_tok[order], flat_w[order]
    counts = jnp.bincount(flat_e, length=N_EXPERTS)
    starts = jnp.cumsum(counts) - counts
    padded = (counts + EXPERT_BLOCK - 1) // EXPERT_BLOCK * EXPERT_BLOCK
    pad_end = jnp.cumsum(padded)
    pad_start = pad_end - padded
    dest = pad_start[s_e] + (jnp.arange(n_assign) - starts[s_e])
    n_rows = ((n_assign + EXPERT_BLOCK - 1) // EXPERT_BLOCK + N_EXPERTS) * EXPERT_BLOCK
    n_blocks = n_rows // EXPERT_BLOCK
    xd = jnp.zeros((n_rows, d), t.dtype).at[dest].set(t[s_tok])
    block_e = jnp.clip(jnp.searchsorted(pad_end, jnp.arange(n_blocks) * EXPERT_BLOCK, side='right'),
                       0, N_EXPERTS - 1)

    def expert_block(args):
        xb, e = args
        return (jax.nn.silu(xb @ w_gate[e]) * (xb @ w_up[e])) @ w_down[e]

    yd = lax.map(expert_block, (xd.reshape(n_blocks, EXPERT_BLOCK, d), block_e)).reshape(n_rows, d)
    y_assign = yd[dest] * s_w.astype(yd.dtype)[:, None]
    y = jax.ops.segment_sum(y_assign, s_tok, num_segments=n_tok)
    return y.reshape(b, s, d)


def setup_inputs(seed: int = 0) -> dict:
    key = jax.random.key(seed)
    ks = jax.random.split(key, 21)
    f32 = jnp.float32

    def normal(k, shape, scale):
        return jax.random.normal(k, shape, f32) * scale

    def gain(k, shape):
        return 1.0 + 0.02 * jax.random.normal(k, shape, f32)

    return {
        "x": normal(ks[0], (BATCH, SEQ, D_MODEL), 1.0),
        "g_norm_mix": gain(ks[1], (DEPTH, D_MODEL)),
        "w_in": normal(ks[2], (DEPTH, D_MODEL, IN_WIDTH), D_MODEL ** -0.5),
        "lambda_q1": normal(ks[3], (DEPTH, DIFF_QK_DIM), 0.1),
        "lambda_k1": normal(ks[4], (DEPTH, DIFF_QK_DIM), 0.1),
        "lambda_q2": normal(ks[5], (DEPTH, DIFF_QK_DIM), 0.1),
        "lambda_k2": normal(ks[6], (DEPTH, DIFF_QK_DIM), 0.1),
        "g_subln": gain(ks[7], (DEPTH, DIFF_V_DIM)),
        "w_up_a": normal(ks[8], (DEPTH, SB_WIDTH, D_MODEL), SB_WIDTH ** -0.5),
        "w_up_b": normal(ks[9], (DEPTH, DIFF_V_WIDTH, D_MODEL), DIFF_V_WIDTH ** -0.5),
        "w_out": normal(ks[10], (DEPTH, D_MODEL, D_MODEL), D_MODEL ** -0.5),
        "g_norm_ffn": gain(ks[11], (DEPTH, D_MODEL)),
        "w_router_group": normal(ks[12], (DEPTH, D_MODEL, N_GROUPS), D_MODEL ** -0.5),
        "b_router_group": normal(ks[13], (DEPTH, N_GROUPS), 0.01),
        "w_router_expert": normal(ks[14], (DEPTH, D_MODEL, N_EXPERTS), D_MODEL ** -0.5),
        "b_router_expert": normal(ks[15], (DEPTH, N_EXPERTS), 0.01),
        "w_expert_gate": normal(ks[16], (DEPTH, N_EXPERTS, D_MODEL, D_EXPERT), D_MODEL ** -0.5),
        "w_expert_up": normal(ks[17], (DEPTH, N_EXPERTS, D_MODEL, D_EXPERT), D_MODEL ** -0.5),
        "w_expert_down": normal(ks[18], (DEPTH, N_EXPERTS, D_EXPERT, D_MODEL), D_EXPERT ** -0.5),
        "g_norm_final": gain(ks[19], (D_MODEL,)),
    }


def reference(x, g_norm_mix, w_in, lambda_q1, lambda_k1, lambda_q2, lambda_k2, g_subln,
              w_up_a, w_up_b, w_out, g_norm_ffn, w_router_group, b_router_group,
              w_router_expert, b_router_expert, w_expert_gate, w_expert_up, w_expert_down,
              g_norm_final):
    b, s, _ = x.shape
    cos, sin = rope_tables(s, DIFF_QK_DIM)
    for l in range(DEPTH):
        h = rmsnorm(x, g_norm_mix[l])
        proj = jnp.einsum('bsd,de->bse', h, w_in[l])
        pieces = []
        off = 0
        for width in IN_SPLITS:
            pieces.append(proj[..., off:off + width])
            off += width
        sb_q, sb_k, sb_v, d_q, d_k, d_v, gate_a, gate_b = pieces

        o_a = stick_breaking_attention(split_heads(sb_q, SB_HEADS, HEAD_DIM),
                                       split_heads(sb_k, SB_HEADS, HEAD_DIM),
                                       split_heads(sb_v, SB_HEADS, HEAD_DIM))

        dq = d_q.reshape(b, s, DIFF_HEADS, 2, DIFF_QK_DIM).transpose(0, 2, 3, 1, 4)
        dk = d_k.reshape(b, s, DIFF_HEADS, 2, DIFF_QK_DIM).transpose(0, 2, 3, 1, 4)
        q1, q2 = apply_rope(dq[:, :, 0], cos, sin), apply_rope(dq[:, :, 1], cos, sin)
        k1, k2 = apply_rope(dk[:, :, 0], cos, sin), apply_rope(dk[:, :, 1], cos, sin)
        dv = split_heads(d_v, DIFF_HEADS, DIFF_V_DIM)
        lambda_init = 0.8 - 0.6 * math.exp(-0.3 * l)
        lam = (jnp.exp(jnp.sum(lambda_q1[l] * lambda_k1[l]))
               - jnp.exp(jnp.sum(lambda_q2[l] * lambda_k2[l])) + lambda_init)
        o_b = differential_attention(q1, q2, k1, k2, dv, lam)
        o_b = rmsnorm(o_b, g_subln[l]) * (1.0 - lambda_init)

        u_a = merge_heads(o_a) @ w_up_a[l]
        u_b = merge_heads(o_b) @ w_up_b[l]
        y = jax.nn.sigmoid(gate_a) * u_a + jax.nn.sigmoid(gate_b) * u_b
        x = x + y @ w_out[l]

        h2 = rmsnorm(x, g_norm_ffn[l])
        x = x + hierarchical_moe(h2, w_router_group[l], b_router_group[l],
                                 w_router_expert[l], b_router_expert[l],
                                 w_expert_gate[l], w_expert_up[l], w_expert_down[l])
    return rmsnorm(x, g_norm_final)
```

```python
import functools
import math

import jax
import jax.numpy as jnp
from jax import lax
from jax.experimental import pallas as pl
from jax.experimental.pallas import tpu as pltpu

D_MODEL = 1024
HEAD_DIM = 64
SB_HEADS = 8
SB_WIDTH = SB_HEADS * HEAD_DIM
DIFF_HEADS = 4
DIFF_QK_DIM = 64
DIFF_V_DIM = 2 * DIFF_QK_DIM
DIFF_QK_WIDTH = DIFF_HEADS * 2 * DIFF_QK_DIM
DIFF_V_WIDTH = DIFF_HEADS * DIFF_V_DIM
IN_WIDTH = 3 * SB_WIDTH + 2 * DIFF_QK_WIDTH + DIFF_V_WIDTH + 2 * D_MODEL
ROPE_THETA = 10000.0
N_GROUPS = 4
EXPERTS_PER_GROUP = 8
N_EXPERTS = N_GROUPS * EXPERTS_PER_GROUP
TOP_K = 2
D_EXPERT = 256
EPS = 1e-6

LANES = 128
QK_SCALE = HEAD_DIM ** -0.5

F32 = jnp.float32
BF16 = jnp.bfloat16

PROJ_ROWS = 512
PROJ_CHUNK = 512
ATT_BLOCK = 256
POST_ROWS = 512
MOE_BLOCK = 256
ROW_DMA_TOKENS = 256
VMEM_LIMIT = 56 * 1024 * 1024

SBQ_CB, SBK_CB, SBV_CB = 0, 4, 8
DQ_CB, DK_CB, DV_CB = 12, 16, 20
GATE_A_B1024, GATE_B_B1024 = 3, 4

SLAB_E0, SLAB_E1, SLAB_R0, SLAB_R1, SLAB_W0, SLAB_W1 = 0, 1, 2, 3, 4, 5
ROUTER_EXPERT_LANE0 = N_GROUPS


def _nt_dot(a, b):
    return lax.dot_general(a, b, (((1,), (1,)), ((), ())), preferred_element_type=F32)


def _inproj_kernel(x_ref, g_ref, w_ref, cos_ref, sin_ref, o_ref):
    x = x_ref[...]
    ms = jnp.mean(x * x, axis=-1, keepdims=True)
    h = (x * lax.rsqrt(ms + EPS) * g_ref[...]).astype(BF16)
    lane = lax.broadcasted_iota(jnp.int32, (PROJ_ROWS, LANES), 1)
    first_half = (lane % DIFF_QK_DIM) < (DIFF_QK_DIM // 2)
    cos = cos_ref[...]
    sin = sin_ref[...]

    def rope(acc):
        outs = []
        for s in range(PROJ_CHUNK // LANES):
            a = acc[:, s * LANES:(s + 1) * LANES]
            swapped = jnp.where(first_half,
                                pltpu.roll(a, LANES - DIFF_QK_DIM // 2, 1),
                                pltpu.roll(a, DIFF_QK_DIM // 2, 1))
            outs.append(a * cos + swapped * sin)
        return jnp.concatenate(outs, axis=1)

    for c in range(IN_WIDTH // PROJ_CHUNK):
        lo = c * PROJ_CHUNK
        acc = jnp.dot(h, w_ref[:, lo:lo + PROJ_CHUNK], preferred_element_type=F32)
        if lo == 0:
            acc = acc * QK_SCALE
        elif lo == DQ_CB * LANES:
            acc = rope(acc) * QK_SCALE
        elif lo == DK_CB * LANES:
            acc = rope(acc)
        o_ref[:, lo:lo + PROJ_CHUNK] = acc.astype(BF16)


def _in_projection(x2, g, w_bf, cos_t, sin_t, seq):
    n = x2.shape[0]
    pos_blocks = seq // PROJ_ROWS
    return pl.pallas_call(
        _inproj_kernel,
        grid=(n // PROJ_ROWS,),
        in_specs=[
            pl.BlockSpec((PROJ_ROWS, D_MODEL), lambda i: (i, 0)),
            pl.BlockSpec((1, D_MODEL), lambda i: (0, 0)),
            pl.BlockSpec((D_MODEL, IN_WIDTH), lambda i: (0, 0)),
            pl.BlockSpec((PROJ_ROWS, LANES), lambda i: (i % pos_blocks, 0)),
            pl.BlockSpec((PROJ_ROWS, LANES), lambda i: (i % pos_blocks, 0)),
        ],
        out_specs=pl.BlockSpec((PROJ_ROWS, IN_WIDTH), lambda i: (i, 0)),
        out_shape=jax.ShapeDtypeStruct((n, IN_WIDTH), BF16),
        compiler_params=pltpu.CompilerParams(
            dimension_semantics=("parallel",), vmem_limit_bytes=VMEM_LIMIT),
        name="in_projection",
    )(x2, g, w_bf, cos_t, sin_t)


def _sb_kernel(q_ref, k_ref, v_ref, o_ref):
    i = pl.program_id(2)
    q = q_ref[...]
    lane = lax.broadcasted_iota(jnp.int32, (ATT_BLOCK, LANES), 1)
    row = lax.broadcasted_iota(jnp.int32, (ATT_BLOCK, ATT_BLOCK), 0)
    col = lax.broadcasted_iota(jnp.int32, (ATT_BLOCK, ATT_BLOCK), 1)
    strict = col < row
    suffix = jnp.where(row >= col, 1.0, 0.0).astype(BF16)
    zero = jnp.zeros_like(q)
    qs = (jnp.where(lane < HEAD_DIM, q, zero), jnp.where(lane >= HEAD_DIM, q, zero))

    def step(j, carry, masked):
        start = pl.multiple_of(j * ATT_BLOCK, ATT_BLOCK)
        kb = k_ref[pl.ds(start, ATT_BLOCK), :]
        vb = v_ref[pl.ds(start, ATT_BLOCK), :]
        out = []
        for h in range(2):
            run, acc = carry[h]
            z = _nt_dot(qs[h], kb)
            l = -(jnp.maximum(z, 0.0) + jnp.log(1.0 + jnp.exp(-jnp.abs(z))))
            if masked:
                l = jnp.where(strict, l, 0.0)
            l_hi = l.astype(BF16)
            l_lo = (l - l_hi.astype(F32)).astype(BF16)
            inc = (jnp.dot(l_hi, suffix, preferred_element_type=F32)
                   + jnp.dot(l_lo, suffix, preferred_element_type=F32))
            a = jnp.exp(z + inc + run)
            if masked:
                a = jnp.where(strict, a, 0.0)
            acc = acc + jnp.dot(a.astype(BF16), vb, preferred_element_type=F32)
            run = run + inc[:, 0:1]
            out.append((run, acc))
        return tuple(out)

    init = tuple((jnp.zeros((ATT_BLOCK, 1), F32), jnp.zeros((ATT_BLOCK, LANES), F32))
                 for _ in range(2))
    carry = step(i, init, True)
    carry = lax.fori_loop(0, i, lambda jj, c: step(i - 1 - jj, c, False), carry)
    o_ref[...] = jnp.where(lane < HEAD_DIM, carry[0][1], carry[1][1]).astype(o_ref.dtype)


def _sb_attention(proj, batch, seq):
    nq = seq // ATT_BLOCK
    return pl.pallas_call(
        _sb_kernel,
        grid=(batch, SB_WIDTH // LANES, nq),
        in_specs=[
            pl.BlockSpec((ATT_BLOCK, LANES), lambda b, p, i: (b * nq + i, SBQ_CB + p)),
            pl.BlockSpec((seq, LANES), lambda b, p, i: (b, SBK_CB + p)),
            pl.BlockSpec((seq, LANES), lambda b, p, i: (b, SBV_CB + p)),
        ],
        out_specs=pl.BlockSpec((ATT_BLOCK, LANES), lambda b, p, i: (b * nq + i, p)),
        out_shape=jax.ShapeDtypeStruct((batch * seq, SB_WIDTH), BF16),
        compiler_params=pltpu.CompilerParams(
            dimension_semantics=("parallel", "parallel", "arbitrary"),
            vmem_limit_bytes=VMEM_LIMIT),
        name="stickbreak_attention",
    )(proj, proj, proj)


def _diff_kernel(lam_ref, gs_ref, q_ref, k_ref, v_ref, o_ref, *, lambda_init):
    i = pl.program_id(2)
    q = q_ref[...]
    lane = lax.broadcasted_iota(jnp.int32, (ATT_BLOCK, LANES), 1)
    row = lax.broadcasted_iota(jnp.int32, (ATT_BLOCK, ATT_BLOCK), 0)
    col = lax.broadcasted_iota(jnp.int32, (ATT_BLOCK, ATT_BLOCK), 1)
    causal = col <= row
    zero = jnp.zeros_like(q)
    qs = (jnp.where(lane < DIFF_QK_DIM, q, zero), jnp.where(lane >= DIFF_QK_DIM, q, zero))

    lv = lam_ref[...]
    lam = (jnp.exp(jnp.sum(lv[0:1] * lv[1:2], axis=-1, keepdims=True))
           - jnp.exp(jnp.sum(lv[2:3] * lv[3:4], axis=-1, keepdims=True)) + lambda_init)

    def step(j, carry, masked):
        start = pl.multiple_of(j * ATT_BLOCK, ATT_BLOCK)
        kb = k_ref[pl.ds(start, ATT_BLOCK), :]
        vb = v_ref[pl.ds(start, ATT_BLOCK), :]
        out = []
        for c in range(2):
            m, l, acc = carry[c]
            s = _nt_dot(qs[c], kb)
            if masked:
                s = jnp.where(causal, s, -jnp.inf)
            m_new = jnp.maximum(m, jnp.max(s, axis=-1, keepdims=True))
            alpha = jnp.exp(m - m_new)
            p = jnp.exp(s - m_new)
            l = alpha * l + jnp.sum(p, axis=-1, keepdims=True)
            acc = alpha * acc + jnp.dot(p.astype(BF16), vb, preferred_element_type=F32)
            out.append((m_new, l, acc))
        return tuple(out)

    init = tuple((jnp.full((ATT_BLOCK, 1), -jnp.inf, F32), jnp.zeros((ATT_BLOCK, 1), F32),
                  jnp.zeros((ATT_BLOCK, LANES), F32)) for _ in range(2))
    carry = step(i, init, True)
    carry = lax.fori_loop(0, i, lambda j, c: step(j, c, False), carry)
    (_, l1, a1), (_, l2, a2) = carry
    o = a1 / l1 - lam * (a2 / l2)
    ms = jnp.mean(o * o, axis=-1, keepdims=True)
    o = o * lax.rsqrt(ms + EPS) * gs_ref[...] * (1.0 - lambda_init)
    o_ref[...] = o.astype(o_ref.dtype)


def _diff_attention(proj, lam_rows, g_subln, batch, seq, lambda_init):
    nq = seq // ATT_BLOCK
    return pl.pallas_call(
        functools.partial(_diff_kernel, lambda_init=lambda_init),
        grid=(batch, DIFF_HEADS, nq),
        in_specs=[
            pl.BlockSpec((4, DIFF_QK_DIM), lambda b, h, i: (0, 0)),
            pl.BlockSpec((1, DIFF_V_DIM), lambda b, h, i: (0, 0)),
            pl.BlockSpec((ATT_BLOCK, LANES), lambda b, h, i: (b * nq + i, DQ_CB + h)),
            pl.BlockSpec((seq, LANES), lambda b, h, i: (b, DK_CB + h)),
            pl.BlockSpec((seq, LANES), lambda b, h, i: (b, DV_CB + h)),
        ],
        out_specs=pl.BlockSpec((ATT_BLOCK, LANES), lambda b, h, i: (b * nq + i, h)),
        out_shape=jax.ShapeDtypeStruct((batch * seq, DIFF_V_WIDTH), BF16),
        compiler_params=pltpu.CompilerParams(
            dimension_semantics=("parallel", "parallel", "arbitrary"),
            vmem_limit_bytes=VMEM_LIMIT),
        name="differential_attention",
    )(lam_rows, g_subln, proj, proj, proj)


def _post_kernel(oa_ref, ob_ref, ga_ref, gb_ref, x_ref, wua_ref, wub_ref, wo_ref, gn_ref,
                 wr_hi_ref, wr_lo_ref, br_ref,
                 x1_ref, h2_ref, slab_ref, cnt_ref, carry_ref):
    step_id = pl.program_id(0)

    @pl.when(step_id == 0)
    def _():
        carry_ref[...] = jnp.zeros_like(carry_ref)

    u_a = jnp.dot(oa_ref[...], wua_ref[...], preferred_element_type=F32)
    u_b = jnp.dot(ob_ref[...], wub_ref[...], preferred_element_type=F32)
    y = (jax.nn.sigmoid(ga_ref[...].astype(F32)) * u_a
         + jax.nn.sigmoid(gb_ref[...].astype(F32)) * u_b)
    x1 = x_ref[...] + jnp.dot(y.astype(BF16), wo_ref[...], preferred_element_type=F32)
    x1_ref[...] = x1
    ms = jnp.mean(x1 * x1, axis=-1, keepdims=True)
    h2 = x1 * lax.rsqrt(ms + EPS) * gn_ref[...]
    h2_ref[...] = h2

    h_hi = h2.astype(BF16)
    h_lo = (h2 - h_hi.astype(F32)).astype(BF16)
    logits = (jnp.dot(h_hi, wr_hi_ref[...], preferred_element_type=F32)
              + jnp.dot(h_lo, wr_hi_ref[...], preferred_element_type=F32)
              + jnp.dot(h_hi, wr_lo_ref[...], preferred_element_type=F32)
              + br_ref[...])

    lane = lax.broadcasted_iota(jnp.int32, (POST_ROWS, LANES), 1)
    neg = -jnp.inf

    def first_argmax(v):
        m = jnp.max(v, axis=-1, keepdims=True)
        idx = jnp.min(jnp.where(v == m, lane, LANES), axis=-1, keepdims=True)
        return m, idx

    g_logits = jnp.where(lane < N_GROUPS, logits, neg)
    g_max, grp = first_argmax(g_logits)
    p_grp = 1.0 / jnp.sum(jnp.exp(g_logits - g_max), axis=-1, keepdims=True)

    lo_lane = ROUTER_EXPERT_LANE0 + grp * EXPERTS_PER_GROUP
    in_grp = (lane >= lo_lane) & (lane < lo_lane + EXPERTS_PER_GROUP)
    e_logits = jnp.where(in_grp, logits, neg)
    m1, i1 = first_argmax(e_logits)
    m2, i2 = first_argmax(jnp.where(lane == i1, neg, e_logits))
    r = jnp.exp(m2 - m1)
    w0 = p_grp / (1.0 + r)
    w1 = p_grp * r / (1.0 + r)

    hit0 = lane == i1
    hit1 = lane == i2
    onehot = jnp.where(hit0 | hit1, 1.0, 0.0)
    trow = lax.broadcasted_iota(jnp.int32, (POST_ROWS, POST_ROWS), 0)
    tcol = lax.broadcasted_iota(jnp.int32, (POST_ROWS, POST_ROWS), 1)
    earlier = jnp.where(tcol < trow, 1.0, 0.0).astype(BF16)
    before = carry_ref[0:1, :] + jnp.dot(earlier, onehot.astype(BF16),
                                         preferred_element_type=F32)
    r0 = jnp.sum(jnp.where(hit0, before, 0.0), axis=-1, keepdims=True)
    r1 = jnp.sum(jnp.where(hit1, before, 0.0), axis=-1, keepdims=True)
    new_carry = carry_ref[0:1, :] + jnp.sum(onehot, axis=0, keepdims=True)
    carry_ref[0:1, :] = new_carry
    cnt_ref[...] = jnp.broadcast_to(new_carry, cnt_ref.shape)

    e0 = (i1 - ROUTER_EXPERT_LANE0).astype(F32)
    e1 = (i2 - ROUTER_EXPERT_LANE0).astype(F32)
    slab = jnp.zeros((POST_ROWS, LANES), F32)
    for pos, val in ((SLAB_E0, e0), (SLAB_E1, e1), (SLAB_R0, r0), (SLAB_R1, r1),
                     (SLAB_W0, w0), (SLAB_W1, w1)):
        slab = jnp.where(lane == pos, val, slab)
    slab_ref[...] = slab


def _post_attention(o_a, o_b, proj, x2, wua, wub, wo, gn, wr_hi, wr_lo, br):
    n = x2.shape[0]
    const = lambda i: (0, 0)
    return pl.pallas_call(
        _post_kernel,
        grid=(n // POST_ROWS,),
        in_specs=[
            pl.BlockSpec((POST_ROWS, SB_WIDTH), lambda i: (i, 0)),
            pl.BlockSpec((POST_ROWS, DIFF_V_WIDTH), lambda i: (i, 0)),
            pl.BlockSpec((POST_ROWS, D_MODEL), lambda i: (i, GATE_A_B1024)),
            pl.BlockSpec((POST_ROWS, D_MODEL), lambda i: (i, GATE_B_B1024)),
            pl.BlockSpec((POST_ROWS, D_MODEL), lambda i: (i, 0)),
            pl.BlockSpec((SB_WIDTH, D_MODEL), const),
            pl.BlockSpec((DIFF_V_WIDTH, D_MODEL), const),
            pl.BlockSpec((D_MODEL, D_MODEL), const),
            pl.BlockSpec((1, D_MODEL), const),
            pl.BlockSpec((D_MODEL, LANES), const),
            pl.BlockSpec((D_MODEL, LANES), const),
            pl.BlockSpec((1, LANES), const),
        ],
        out_specs=[
            pl.BlockSpec((POST_ROWS, D_MODEL), lambda i: (i, 0)),
            pl.BlockSpec((POST_ROWS, D_MODEL), lambda i: (i, 0)),
            pl.BlockSpec((POST_ROWS, LANES), lambda i: (i, 0)),
            pl.BlockSpec((8, LANES), const),
        ],
        out_shape=[
            jax.ShapeDtypeStruct((n, D_MODEL), F32),
            jax.ShapeDtypeStruct((n, D_MODEL), F32),
            jax.ShapeDtypeStruct((n, LANES), F32),
            jax.ShapeDtypeStruct((8, LANES), F32),
        ],
        scratch_shapes=[pltpu.VMEM((8, LANES), F32)],
        compiler_params=pltpu.CompilerParams(
            dimension_semantics=("arbitrary",), vmem_limit_bytes=VMEM_LIMIT),
        name="merge_route",
    )(o_a, o_b, proj, proj, x2, wua, wub, wo, gn, wr_hi, wr_lo, br)


def _row_copy(src_ref, src_row, dst_ref, dst_row, sem):
    return pltpu.make_async_copy(src_ref.at[src_row], dst_ref.at[dst_row], sem)


def _scatter_kernel(dest_ref, h_ref, zeros_ref, xd_ref, sem):
    del zeros_ref
    n_copies = TOP_K * ROW_DMA_TOKENS

    def issue(a, _):
        _row_copy(h_ref, a // TOP_K, xd_ref, dest_ref[0, 0, a], sem).start()
        return 0

    lax.fori_loop(0, n_copies, issue, 0)

    def drain(a, _):
        _row_copy(h_ref, 0, xd_ref, 0, sem).wait()
        return 0

    lax.fori_loop(0, n_copies, drain, 0)


def _scatter_rows(dest3, h2, n_rows):
    n = h2.shape[0]
    return pl.pallas_call(
        _scatter_kernel,
        grid=(n // ROW_DMA_TOKENS,),
        in_specs=[
            pl.BlockSpec((1, 1, TOP_K * ROW_DMA_TOKENS), lambda i: (i, 0, 0),
                         memory_space=pltpu.SMEM),
            pl.BlockSpec((ROW_DMA_TOKENS, D_MODEL), lambda i: (i, 0)),
            pl.BlockSpec(memory_space=pl.ANY),
        ],
        out_specs=pl.BlockSpec(memory_space=pl.ANY),
        out_shape=jax.ShapeDtypeStruct((n_rows, D_MODEL), F32),
        input_output_aliases={2: 0},
        scratch_shapes=[pltpu.SemaphoreType.DMA],
        compiler_params=pltpu.CompilerParams(
            dimension_semantics=("arbitrary",), vmem_limit_bytes=VMEM_LIMIT),
        name="scatter_rows",
    )(dest3, h2, jnp.zeros((n_rows, D_MODEL), F32))


def _expert_kernel(be_ref, nb_ref, x_ref, wg_ref, wu_ref, wd_ref, y_ref):
    @pl.when(pl.program_id(0) < nb_ref[0])
    def _():
        x = x_ref[...].astype(BF16)
        g = jnp.dot(x, wg_ref[0], preferred_element_type=F32)
        u = jnp.dot(x, wu_ref[0], preferred_element_type=F32)
        hmid = (g * jax.nn.sigmoid(g) * u).astype(BF16)
        y_ref[...] = jnp.dot(hmid, wd_ref[0], preferred_element_type=F32)

    @pl.when(pl.program_id(0) >= nb_ref[0])
    def _():
        y_ref[...] = jnp.zeros_like(y_ref)


def _expert_mlp(block_e, n_used, xd, wg, wu, wd):
    n_rows = xd.shape[0]
    n_blocks = n_rows // MOE_BLOCK

    def row_map(i, be, nb):
        return (jnp.minimum(i, nb[0] - 1), 0)

    def w_map(i, be, nb):
        return (be[i], 0, 0)

    return pl.pallas_call(
        _expert_kernel,
        grid_spec=pltpu.PrefetchScalarGridSpec(
            num_scalar_prefetch=2,
            grid=(n_blocks,),
            in_specs=[
                pl.BlockSpec((MOE_BLOCK, D_MODEL), row_map),
                pl.BlockSpec((1, D_MODEL, D_EXPERT), w_map),
                pl.BlockSpec((1, D_MODEL, D_EXPERT), w_map),
                pl.BlockSpec((1, D_EXPERT, D_MODEL), w_map),
            ],
            out_specs=pl.BlockSpec((MOE_BLOCK, D_MODEL), lambda i, be, nb: (i, 0)),
        ),
        out_shape=jax.ShapeDtypeStruct((n_rows, D_MODEL), F32),
        compiler_params=pltpu.CompilerParams(
            dimension_semantics=("arbitrary",), vmem_limit_bytes=VMEM_LIMIT),
        name="expert_mlp",
    )(block_e, n_used, xd, wg, wu, wd)


def _combine_kernel(dest_ref, x1_ref, slab_ref, g_ref, yd_ref, o_ref, ybuf, sem):
    n_copies = TOP_K * ROW_DMA_TOKENS

    def issue(a, _):
        slot = (a % TOP_K) * ROW_DMA_TOKENS + a // TOP_K
        _row_copy(yd_ref, dest_ref[0, 0, a], ybuf, slot, sem).start()
        return 0

    lax.fori_loop(0, n_copies, issue, 0)

    def drain(a, _):
        _row_copy(yd_ref, 0, ybuf, 0, sem).wait()
        return 0

    lax.fori_loop(0, n_copies, drain, 0)

    slab = slab_ref[...]
    w0 = slab[:, SLAB_W0:SLAB_W0 + 1]
    w1 = slab[:, SLAB_W1:SLAB_W1 + 1]
    x = (x1_ref[...] + w0 * ybuf[0:ROW_DMA_TOKENS, :]
         + w1 * ybuf[ROW_DMA_TOKENS:TOP_K * ROW_DMA_TOKENS, :])
    ms = jnp.mean(x * x, axis=-1, keepdims=True)
    o_ref[...] = x * lax.rsqrt(ms + EPS) * g_ref[...]


def _combine(dest3, x1, slab, g_final, yd):
    n = x1.shape[0]
    return pl.pallas_call(
        _combine_kernel,
        grid=(n // ROW_DMA_TOKENS,),
        in_specs=[
            pl.BlockSpec((1, 1, TOP_K * ROW_DMA_TOKENS), lambda i: (i, 0, 0),
                         memory_space=pltpu.SMEM),
            pl.BlockSpec((ROW_DMA_TOKENS, D_MODEL), lambda i: (i, 0)),
            pl.BlockSpec((ROW_DMA_TOKENS, LANES), lambda i: (i, 0)),
            pl.BlockSpec((1, D_MODEL), lambda i: (0, 0)),
            pl.BlockSpec(memory_space=pl.ANY),
        ],
        out_specs=pl.BlockSpec((ROW_DMA_TOKENS, D_MODEL), lambda i: (i, 0)),
        out_shape=jax.ShapeDtypeStruct((n, D_MODEL), F32),
        scratch_shapes=[pltpu.VMEM((TOP_K * ROW_DMA_TOKENS, D_MODEL), F32),
                        pltpu.SemaphoreType.DMA],
        compiler_params=pltpu.CompilerParams(
            dimension_semantics=("arbitrary",), vmem_limit_bytes=VMEM_LIMIT),
        name="combine_norm",
    )(dest3, x1, slab, g_final, yd)


def _rope_tables(seq):
    pos = jnp.arange(seq, dtype=F32)
    inv_freq = 1.0 / (ROPE_THETA ** (jnp.arange(0, DIFF_QK_DIM, 2, dtype=F32) / DIFF_QK_DIM))
    freqs = pos[:, None] * inv_freq[None, :]
    emb = jnp.concatenate([freqs, freqs], axis=-1)
    cos, sin = jnp.cos(emb), jnp.sin(emb)
    half = DIFF_QK_DIM // 2
    sin_signed = jnp.concatenate([-sin[:, :half], sin[:, half:]], axis=-1)
    reps = LANES // DIFF_QK_DIM
    return jnp.tile(cos, (1, reps)), jnp.tile(sin_signed, (1, reps))


def _split_bf16(w):
    hi = w.astype(BF16)
    lo = (w - hi.astype(F32)).astype(BF16)
    return hi, lo


def kernel(x, g_norm_mix, w_in, lambda_q1, lambda_k1, lambda_q2, lambda_k2, g_subln,
           w_up_a, w_up_b, w_out, g_norm_ffn, w_router_group, b_router_group,
           w_router_expert, b_router_expert, w_expert_gate, w_expert_up, w_expert_down,
           g_norm_final):
    batch, seq, d = x.shape
    depth = w_in.shape[0]
    n_tok = batch * seq
    assert depth == 1, "the final RMSNorm is fused into the layer's combine step"
    assert d == D_MODEL and seq % ATT_BLOCK == 0 and seq % PROJ_ROWS == 0
    assert n_tok % POST_ROWS == 0 and n_tok % ROW_DMA_TOKENS == 0
    n_assign = n_tok * TOP_K
    n_blocks = n_assign // MOE_BLOCK + N_EXPERTS
    n_rows = n_blocks * MOE_BLOCK

    cos_t, sin_t = _rope_tables(seq)
    x2 = x.reshape(n_tok, d)
    for l in range(depth):
        lambda_init = 0.8 - 0.6 * math.exp(-0.3 * l)
        proj = _in_projection(x2, g_norm_mix[l][None, :], w_in[l].astype(BF16),
                              cos_t, sin_t, seq)
        o_a = _sb_attention(proj, batch, seq)
        lam_rows = jnp.stack([lambda_q1[l], lambda_k1[l], lambda_q2[l], lambda_k2[l]])
        o_b = _diff_attention(proj, lam_rows, g_subln[l][None, :], batch, seq, lambda_init)

        w_r = jnp.zeros((d, LANES), F32)
        w_r = w_r.at[:, :N_GROUPS].set(w_router_group[l])
        w_r = w_r.at[:, N_GROUPS:N_GROUPS + N_EXPERTS].set(w_router_expert[l])
        b_r = jnp.zeros((1, LANES), F32)
        b_r = b_r.at[0, :N_GROUPS].set(b_router_group[l])
        b_r = b_r.at[0, N_GROUPS:N_GROUPS + N_EXPERTS].set(b_router_expert[l])
        wr_hi, wr_lo = _split_bf16(w_r)

        x1, h2, slab, cnt = _post_attention(
            o_a, o_b, proj, x2, w_up_a[l].astype(BF16), w_up_b[l].astype(BF16),
            w_out[l].astype(BF16), g_norm_ffn[l][None, :], wr_hi, wr_lo, b_r)

        counts = cnt[0, N_GROUPS:N_GROUPS + N_EXPERTS].astype(jnp.int32)
        padded = (counts + MOE_BLOCK - 1) // MOE_BLOCK * MOE_BLOCK
        pad_end = jnp.cumsum(padded)
        pad_start = pad_end - padded
        e_ids = slab[:, SLAB_E0:SLAB_E1 + 1].astype(jnp.int32)
        ranks = slab[:, SLAB_R0:SLAB_R1 + 1].astype(jnp.int32)
        dest = pad_start[e_ids] + ranks
        dest3 = dest.reshape(n_tok // ROW_DMA_TOKENS, 1, TOP_K * ROW_DMA_TOKENS)
        block_start = jnp.arange(n_blocks, dtype=jnp.int32) * MOE_BLOCK
        block_e = jnp.minimum(
            jnp.sum((pad_end[None, :] <= block_start[:, None]).astype(jnp.int32), axis=1),
            N_EXPERTS - 1)
        n_used = (pad_end[-1:] // MOE_BLOCK).astype(jnp.int32)

        xd = _scatter_rows(dest3, h2, n_rows)
        yd = _expert_mlp(block_e, n_used, xd, w_expert_gate[l].astype(BF16),
                         w_expert_up[l].astype(BF16), w_expert_down[l].astype(BF16))
        x2 = _combine(dest3, x1, slab, g_norm_final[None, :], yd)
    return x2.reshape(batch, seq, d)
```

```python
import functools
import math

import jax
import jax.numpy as jnp
from jax import lax
from jax.experimental import pallas as pl
from jax.experimental.pallas import tpu as pltpu

D_MODEL = 1024
HEAD_DIM = 64
SB_HEADS = 8
SB_WIDTH = SB_HEADS * HEAD_DIM
DIFF_HEADS = 4
DIFF_QK_DIM = 64
DIFF_V_DIM = 2 * DIFF_QK_DIM
DIFF_QK_WIDTH = DIFF_HEADS * 2 * DIFF_QK_DIM
DIFF_V_WIDTH = DIFF_HEADS * DIFF_V_DIM
IN_WIDTH = 3 * SB_WIDTH + 2 * DIFF_QK_WIDTH + DIFF_V_WIDTH + 2 * D_MODEL
ROPE_THETA = 10000.0
N_GROUPS = 4
EXPERTS_PER_GROUP = 8
N_EXPERTS = N_GROUPS * EXPERTS_PER_GROUP
TOP_K = 2
D_EXPERT = 256
EPS = 1e-6

LANES = 128
QK_SCALE = HEAD_DIM ** -0.5

F32 = jnp.float32
BF16 = jnp.bfloat16

PROJ_ROWS = 512
PROJ_CHUNK = 512
SB_BLOCK = 256
ATT_BLOCK = 512
SB_LOG_UNDERFLOW = -105.0
POST_ROWS = 512
MOE_BLOCK = 256
ROW_DMA_TOKENS = 256
VMEM_LIMIT = 56 * 1024 * 1024

SBQ_CB, SBK_CB, SBV_CB = 0, 4, 8
DQ_CB, DK_CB, DV_CB = 12, 16, 20
GATE_A_B1024, GATE_B_B1024 = 3, 4

SLAB_E0, SLAB_E1, SLAB_R0, SLAB_R1, SLAB_W0, SLAB_W1 = 0, 1, 2, 3, 4, 5
ROUTER_EXPERT_LANE0 = N_GROUPS


def _nt_dot(a, b):
    return lax.dot_general(a, b, (((1,), (1,)), ((), ())), preferred_element_type=F32)


def _inproj_kernel(x_ref, g_ref, w_ref, cos_ref, sin_ref, o_ref):
    x = x_ref[...]
    ms = jnp.mean(x * x, axis=-1, keepdims=True)
    h = (x * lax.rsqrt(ms + EPS) * g_ref[...]).astype(BF16)
    lane = lax.broadcasted_iota(jnp.int32, (PROJ_ROWS, LANES), 1)
    first_half = (lane % DIFF_QK_DIM) < (DIFF_QK_DIM // 2)
    cos = cos_ref[...]
    sin = sin_ref[...]

    def rope(acc):
        outs = []
        for s in range(PROJ_CHUNK // LANES):
            a = acc[:, s * LANES:(s + 1) * LANES]
            swapped = jnp.where(first_half,
                                pltpu.roll(a, LANES - DIFF_QK_DIM // 2, 1),
                                pltpu.roll(a, DIFF_QK_DIM // 2, 1))
            outs.append(a * cos + swapped * sin)
        return jnp.concatenate(outs, axis=1)

    for c in range(IN_WIDTH // PROJ_CHUNK):
        lo = c * PROJ_CHUNK
        acc = jnp.dot(h, w_ref[:, lo:lo + PROJ_CHUNK], preferred_element_type=F32)
        if lo == 0:
            acc = acc * QK_SCALE
        elif lo == DQ_CB * LANES:
            acc = rope(acc) * QK_SCALE
        elif lo == DK_CB * LANES:
            acc = rope(acc)
        o_ref[:, lo:lo + PROJ_CHUNK] = acc.astype(BF16)


def _in_projection(x2, g, w_bf, cos_t, sin_t, seq):
    n = x2.shape[0]
    pos_blocks = seq // PROJ_ROWS
    return pl.pallas_call(
        _inproj_kernel,
        grid=(n // PROJ_ROWS,),
        in_specs=[
            pl.BlockSpec((PROJ_ROWS, D_MODEL), lambda i: (i, 0)),
            pl.BlockSpec((1, D_MODEL), lambda i: (0, 0)),
            pl.BlockSpec((D_MODEL, IN_WIDTH), lambda i: (0, 0)),
            pl.BlockSpec((PROJ_ROWS, LANES), lambda i: (i % pos_blocks, 0)),
            pl.BlockSpec((PROJ_ROWS, LANES), lambda i: (i % pos_blocks, 0)),
        ],
        out_specs=pl.BlockSpec((PROJ_ROWS, IN_WIDTH), lambda i: (i, 0)),
        out_shape=jax.ShapeDtypeStruct((n, IN_WIDTH), BF16),
        compiler_params=pltpu.CompilerParams(
            dimension_semantics=("parallel",), vmem_limit_bytes=VMEM_LIMIT),
        name="in_projection",
    )(x2, g, w_bf, cos_t, sin_t)


def _sb_kernel(q_ref, k_ref, v_ref, o_ref):
    i = pl.program_id(2)
    q = q_ref[...]
    lane = lax.broadcasted_iota(jnp.int32, (SB_BLOCK, LANES), 1)
    row = lax.broadcasted_iota(jnp.int32, (SB_BLOCK, SB_BLOCK), 0)
    col = lax.broadcasted_iota(jnp.int32, (SB_BLOCK, SB_BLOCK), 1)
    strict = col < row
    suffix = jnp.where(row >= col, 1.0, 0.0).astype(BF16)
    zero = jnp.zeros_like(q)
    qs = (jnp.where(lane < HEAD_DIM, q, zero), jnp.where(lane >= HEAD_DIM, q, zero))

    def block(j, qh, run, masked):
        start = pl.multiple_of(j * SB_BLOCK, SB_BLOCK)
        z = _nt_dot(qh, k_ref[pl.ds(start, SB_BLOCK), :])
        l = -(jnp.maximum(z, 0.0) + jnp.log(1.0 + jnp.exp(-jnp.abs(z))))
        if masked:
            l = jnp.where(strict, l, 0.0)
        l_hi = l.astype(BF16)
        l_lo = (l - l_hi.astype(F32)).astype(BF16)
        inc = (jnp.dot(l_hi, suffix, preferred_element_type=F32)
               + jnp.dot(l_lo, suffix, preferred_element_type=F32)) + run
        a = jnp.exp(z + inc)
        if masked:
            a = jnp.where(strict, a, 0.0)
        return a, inc[:, 0:1], start

    def weighted_values(a, start):
        return jnp.dot(a.astype(BF16), v_ref[pl.ds(start, SB_BLOCK), :],
                       preferred_element_type=F32)

    has_prev = i >= 1
    prev = jnp.maximum(i - 1, 0)
    runs, accs = [], []
    for h in range(2):
        a_d, run_d, s_d = block(i, qs[h], jnp.zeros((SB_BLOCK, 1), F32), True)
        a_p, run_p, s_p = block(prev, qs[h], run_d, False)
        a_p = jnp.where(has_prev, a_p, 0.0)
        accs.append(weighted_values(a_d, s_d) + weighted_values(a_p, s_p))
        runs.append(run_p)

    def alive(r0, r1):
        return (jnp.max(jnp.maximum(r0, r1)) > SB_LOG_UNDERFLOW).astype(jnp.int32)

    def cond(c):
        return (c[0] >= 0) & (c[1] > 0)

    def body(c):
        j = c[0]
        new_runs, new_accs = [], []
        for h in range(2):
            a, run, start = block(j, qs[h], c[2 + h], False)
            new_accs.append(c[4 + h] + weighted_values(a, start))
            new_runs.append(run)
        return (j - 1, alive(*new_runs), *new_runs, *new_accs)

    out = lax.while_loop(cond, body, (i - 2, alive(*runs), *runs, *accs))
    o_ref[...] = jnp.where(lane < HEAD_DIM, out[4], out[5]).astype(o_ref.dtype)


def _sb_attention(proj, batch, seq):
    nq = seq // SB_BLOCK
    return pl.pallas_call(
        _sb_kernel,
        grid=(batch, SB_WIDTH // LANES, nq),
        in_specs=[
            pl.BlockSpec((SB_BLOCK, LANES), lambda b, p, i: (b * nq + i, SBQ_CB + p)),
            pl.BlockSpec((seq, LANES), lambda b, p, i: (b, SBK_CB + p)),
            pl.BlockSpec((seq, LANES), lambda b, p, i: (b, SBV_CB + p)),
        ],
        out_specs=pl.BlockSpec((SB_BLOCK, LANES), lambda b, p, i: (b * nq + i, p)),
        out_shape=jax.ShapeDtypeStruct((batch * seq, SB_WIDTH), BF16),
        compiler_params=pltpu.CompilerParams(
            dimension_semantics=("parallel", "parallel", "arbitrary"),
            vmem_limit_bytes=VMEM_LIMIT),
        name="stickbreak_attention",
    )(proj, proj, proj)


def _diff_kernel(lam_ref, gs_ref, q_ref, k_ref, v_ref, o_ref, *, lambda_init):
    i = pl.program_id(2)
    q = q_ref[...]
    lane = lax.broadcasted_iota(jnp.int32, (ATT_BLOCK, LANES), 1)
    row = lax.broadcasted_iota(jnp.int32, (ATT_BLOCK, ATT_BLOCK), 0)
    col = lax.broadcasted_iota(jnp.int32, (ATT_BLOCK, ATT_BLOCK), 1)
    causal = col <= row
    zero = jnp.zeros_like(q)
    qs = (jnp.where(lane < DIFF_QK_DIM, q, zero), jnp.where(lane >= DIFF_QK_DIM, q, zero))

    lv = lam_ref[...]
    lam = (jnp.exp(jnp.sum(lv[0:1] * lv[1:2], axis=-1, keepdims=True))
           - jnp.exp(jnp.sum(lv[2:3] * lv[3:4], axis=-1, keepdims=True)) + lambda_init)

    def step(j, carry, masked):
        start = pl.multiple_of(j * ATT_BLOCK, ATT_BLOCK)
        kb = k_ref[pl.ds(start, ATT_BLOCK), :]
        vb = v_ref[pl.ds(start, ATT_BLOCK), :]
        out = []
        for c in range(2):
            m, l, acc = carry[c]
            s = _nt_dot(qs[c], kb)
            if masked:
                s = jnp.where(causal, s, -jnp.inf)
            m_new = jnp.maximum(m, jnp.max(s, axis=-1, keepdims=True))
            alpha = jnp.exp(m - m_new)
            p = jnp.exp(s - m_new)
            l = alpha * l + jnp.sum(p, axis=-1, keepdims=True)
            acc = alpha * acc + jnp.dot(p.astype(BF16), vb, preferred_element_type=F32)
            out.append((m_new, l, acc))
        return tuple(out)

    init = tuple((jnp.full((ATT_BLOCK, 1), -jnp.inf, F32), jnp.zeros((ATT_BLOCK, 1), F32),
                  jnp.zeros((ATT_BLOCK, LANES), F32)) for _ in range(2))
    carry = step(i, init, True)
    carry = lax.fori_loop(0, i, lambda j, c: step(j, c, False), carry)
    (_, l1, a1), (_, l2, a2) = carry
    o = a1 / l1 - lam * (a2 / l2)
    ms = jnp.mean(o * o, axis=-1, keepdims=True)
    o = o * lax.rsqrt(ms + EPS) * gs_ref[...] * (1.0 - lambda_init)
    o_ref[...] = o.astype(o_ref.dtype)


def _diff_attention(proj, lam_rows, g_subln, batch, seq, lambda_init):
    nq = seq // ATT_BLOCK
    return pl.pallas_call(
        functools.partial(_diff_kernel, lambda_init=lambda_init),
        grid=(batch, DIFF_HEADS, nq),
        in_specs=[
            pl.BlockSpec((4, DIFF_QK_DIM), lambda b, h, i: (0, 0)),
            pl.BlockSpec((1, DIFF_V_DIM), lambda b, h, i: (0, 0)),
            pl.BlockSpec((ATT_BLOCK, LANES), lambda b, h, i: (b * nq + i, DQ_CB + h)),
            pl.BlockSpec((seq, LANES), lambda b, h, i: (b, DK_CB + h)),
            pl.BlockSpec((seq, LANES), lambda b, h, i: (b, DV_CB + h)),
        ],
        out_specs=pl.BlockSpec((ATT_BLOCK, LANES), lambda b, h, i: (b * nq + i, h)),
        out_shape=jax.ShapeDtypeStruct((batch * seq, DIFF_V_WIDTH), BF16),
        compiler_params=pltpu.CompilerParams(
            dimension_semantics=("parallel", "parallel", "arbitrary"),
            vmem_limit_bytes=VMEM_LIMIT),
        name="differential_attention",
    )(lam_rows, g_subln, proj, proj, proj)


def _post_kernel(oa_ref, ob_ref, ga_ref, gb_ref, x_ref, wua_ref, wub_ref, wo_ref, gn_ref,
                 wr_hi_ref, wr_lo_ref, br_ref,
                 x1_ref, h2_ref, slab_ref, cnt_ref, carry_ref):
    step_id = pl.program_id(0)

    @pl.when(step_id == 0)
    def _():
        carry_ref[...] = jnp.zeros_like(carry_ref)

    u_a = jnp.dot(oa_ref[...], wua_ref[...], preferred_element_type=F32)
    u_b = jnp.dot(ob_ref[...], wub_ref[...], preferred_element_type=F32)
    y = (jax.nn.sigmoid(ga_ref[...].astype(F32)) * u_a
         + jax.nn.sigmoid(gb_ref[...].astype(F32)) * u_b)
    x1 = x_ref[...] + jnp.dot(y.astype(BF16), wo_ref[...], preferred_element_type=F32)
    x1_ref[...] = x1
    ms = jnp.mean(x1 * x1, axis=-1, keepdims=True)
    h2 = x1 * lax.rsqrt(ms + EPS) * gn_ref[...]
    h2_ref[...] = h2

    h_hi = h2.astype(BF16)
    h_lo = (h2 - h_hi.astype(F32)).astype(BF16)
    logits = (jnp.dot(h_hi, wr_hi_ref[...], preferred_element_type=F32)
              + jnp.dot(h_lo, wr_hi_ref[...], preferred_element_type=F32)
              + jnp.dot(h_hi, wr_lo_ref[...], preferred_element_type=F32)
              + br_ref[...])

    lane = lax.broadcasted_iota(jnp.int32, (POST_ROWS, LANES), 1)
    neg = -jnp.inf

    def first_argmax(v):
        m = jnp.max(v, axis=-1, keepdims=True)
        idx = jnp.min(jnp.where(v == m, lane, LANES), axis=-1, keepdims=True)
        return m, idx

    g_logits = jnp.where(lane < N_GROUPS, logits, neg)
    g_max, grp = first_argmax(g_logits)
    p_grp = 1.0 / jnp.sum(jnp.exp(g_logits - g_max), axis=-1, keepdims=True)

    lo_lane = ROUTER_EXPERT_LANE0 + grp * EXPERTS_PER_GROUP
    in_grp = (lane >= lo_lane) & (lane < lo_lane + EXPERTS_PER_GROUP)
    e_logits = jnp.where(in_grp, logits, neg)
    m1, i1 = first_argmax(e_logits)
    m2, i2 = first_argmax(jnp.where(lane == i1, neg, e_logits))
    r = jnp.exp(m2 - m1)
    w0 = p_grp / (1.0 + r)
    w1 = p_grp * r / (1.0 + r)

    hit0 = lane == i1
    hit1 = lane == i2
    onehot = jnp.where(hit0 | hit1, 1.0, 0.0)
    trow = lax.broadcasted_iota(jnp.int32, (POST_ROWS, POST_ROWS), 0)
    tcol = lax.broadcasted_iota(jnp.int32, (POST_ROWS, POST_ROWS), 1)
    earlier = jnp.where(tcol < trow, 1.0, 0.0).astype(BF16)
    before = carry_ref[0:1, :] + jnp.dot(earlier, onehot.astype(BF16),
                                         preferred_element_type=F32)
    r0 = jnp.sum(jnp.where(hit0, before, 0.0), axis=-1, keepdims=True)
    r1 = jnp.sum(jnp.where(hit1, before, 0.0), axis=-1, keepdims=True)
    new_carry = carry_ref[0:1, :] + jnp.sum(onehot, axis=0, keepdims=True)
    carry_ref[0:1, :] = new_carry
    cnt_ref[...] = jnp.broadcast_to(new_carry, cnt_ref.shape)

    e0 = (i1 - ROUTER_EXPERT_LANE0).astype(F32)
    e1 = (i2 - ROUTER_EXPERT_LANE0).astype(F32)
    slab = jnp.zeros((POST_ROWS, LANES), F32)
    for pos, val in ((SLAB_E0, e0), (SLAB_E1, e1), (SLAB_R0, r0), (SLAB_R1, r1),
                     (SLAB_W0, w0), (SLAB_W1, w1)):
        slab = jnp.where(lane == pos, val, slab)
    slab_ref[...] = slab


def _post_attention(o_a, o_b, proj, x2, wua, wub, wo, gn, wr_hi, wr_lo, br):
    n = x2.shape[0]
    const = lambda i: (0, 0)
    return pl.pallas_call(
        _post_kernel,
        grid=(n // POST_ROWS,),
        in_specs=[
            pl.BlockSpec((POST_ROWS, SB_WIDTH), lambda i: (i, 0)),
            pl.BlockSpec((POST_ROWS, DIFF_V_WIDTH), lambda i: (i, 0)),
            pl.BlockSpec((POST_ROWS, D_MODEL), lambda i: (i, GATE_A_B1024)),
            pl.BlockSpec((POST_ROWS, D_MODEL), lambda i: (i, GATE_B_B1024)),
            pl.BlockSpec((POST_ROWS, D_MODEL), lambda i: (i, 0)),
            pl.BlockSpec((SB_WIDTH, D_MODEL), const),
            pl.BlockSpec((DIFF_V_WIDTH, D_MODEL), const),
            pl.BlockSpec((D_MODEL, D_MODEL), const),
            pl.BlockSpec((1, D_MODEL), const),
            pl.BlockSpec((D_MODEL, LANES), const),
            pl.BlockSpec((D_MODEL, LANES), const),
            pl.BlockSpec((1, LANES), const),
        ],
        out_specs=[
            pl.BlockSpec((POST_ROWS, D_MODEL), lambda i: (i, 0)),
            pl.BlockSpec((POST_ROWS, D_MODEL), lambda i: (i, 0)),
            pl.BlockSpec((POST_ROWS, LANES), lambda i: (i, 0)),
            pl.BlockSpec((8, LANES), const),
        ],
        out_shape=[
            jax.ShapeDtypeStruct((n, D_MODEL), F32),
            jax.ShapeDtypeStruct((n, D_MODEL), F32),
            jax.ShapeDtypeStruct((n, LANES), F32),
            jax.ShapeDtypeStruct((8, LANES), F32),
        ],
        scratch_shapes=[pltpu.VMEM((8, LANES), F32)],
        compiler_params=pltpu.CompilerParams(
            dimension_semantics=("arbitrary",), vmem_limit_bytes=VMEM_LIMIT),
        name="merge_route",
    )(o_a, o_b, proj, proj, x2, wua, wub, wo, gn, wr_hi, wr_lo, br)


def _row_copy(src_ref, src_row, dst_ref, dst_row, sem):
    return pltpu.make_async_copy(src_ref.at[src_row], dst_ref.at[dst_row], sem)


def _scatter_kernel(dest_ref, h_ref, zeros_ref, xd_ref, sem):
    del zeros_ref
    n_copies = TOP_K * ROW_DMA_TOKENS

    def issue(a, _):
        _row_copy(h_ref, a // TOP_K, xd_ref, dest_ref[0, 0, a], sem).start()
        return 0

    lax.fori_loop(0, n_copies, issue, 0)

    def drain(a, _):
        _row_copy(h_ref, 0, xd_ref, 0, sem).wait()
        return 0

    lax.fori_loop(0, n_copies, drain, 0)


def _scatter_rows(dest3, h2, n_rows):
    n = h2.shape[0]
    return pl.pallas_call(
        _scatter_kernel,
        grid=(n // ROW_DMA_TOKENS,),
        in_specs=[
            pl.BlockSpec((1, 1, TOP_K * ROW_DMA_TOKENS), lambda i: (i, 0, 0),
                         memory_space=pltpu.SMEM),
            pl.BlockSpec((ROW_DMA_TOKENS, D_MODEL), lambda i: (i, 0)),
            pl.BlockSpec(memory_space=pl.ANY),
        ],
        out_specs=pl.BlockSpec(memory_space=pl.ANY),
        out_shape=jax.ShapeDtypeStruct((n_rows, D_MODEL), F32),
        input_output_aliases={2: 0},
        scratch_shapes=[pltpu.SemaphoreType.DMA],
        compiler_params=pltpu.CompilerParams(
            dimension_semantics=("arbitrary",), vmem_limit_bytes=VMEM_LIMIT),
        name="scatter_rows",
    )(dest3, h2, jnp.zeros((n_rows, D_MODEL), F32))


def _expert_kernel(be_ref, nb_ref, x_ref, wg_ref, wu_ref, wd_ref, y_ref):
    @pl.when(pl.program_id(0) < nb_ref[0])
    def _():
        x = x_ref[...].astype(BF16)
        g = jnp.dot(x, wg_ref[0], preferred_element_type=F32)
        u = jnp.dot(x, wu_ref[0], preferred_element_type=F32)
        hmid = (g * jax.nn.sigmoid(g) * u).astype(BF16)
        y_ref[...] = jnp.dot(hmid, wd_ref[0], preferred_element_type=F32)

    @pl.when(pl.program_id(0) >= nb_ref[0])
    def _():
        y_ref[...] = jnp.zeros_like(y_ref)


def _expert_mlp(block_e, n_used, xd, wg, wu, wd):
    n_rows = xd.shape[0]
    n_blocks = n_rows // MOE_BLOCK

    def row_map(i, be, nb):
        return (jnp.minimum(i, nb[0] - 1), 0)

    def w_map(i, be, nb):
        return (be[i], 0, 0)

    return pl.pallas_call(
        _expert_kernel,
        grid_spec=pltpu.PrefetchScalarGridSpec(
            num_scalar_prefetch=2,
            grid=(n_blocks,),
            in_specs=[
                pl.BlockSpec((MOE_BLOCK, D_MODEL), row_map),
                pl.BlockSpec((1, D_MODEL, D_EXPERT), w_map),
                pl.BlockSpec((1, D_MODEL, D_EXPERT), w_map),
                pl.BlockSpec((1, D_EXPERT, D_MODEL), w_map),
            ],
            out_specs=pl.BlockSpec((MOE_BLOCK, D_MODEL), lambda i, be, nb: (i, 0)),
        ),
        out_shape=jax.ShapeDtypeStruct((n_rows, D_MODEL), F32),
        compiler_params=pltpu.CompilerParams(
            dimension_semantics=("arbitrary",), vmem_limit_bytes=VMEM_LIMIT),
        name="expert_mlp",
    )(block_e, n_used, xd, wg, wu, wd)


def _combine_kernel(dest_ref, x1_ref, slab_ref, g_ref, yd_ref, o_ref, ybuf, sem):
    n_copies = TOP_K * ROW_DMA_TOKENS

    def issue(a, _):
        slot = (a % TOP_K) * ROW_DMA_TOKENS + a // TOP_K
        _row_copy(yd_ref, dest_ref[0, 0, a], ybuf, slot, sem).start()
        return 0

    lax.fori_loop(0, n_copies, issue, 0)

    def drain(a, _):
        _row_copy(yd_ref, 0, ybuf, 0, sem).wait()
        return 0

    lax.fori_loop(0, n_copies, drain, 0)

    slab = slab_ref[...]
    w0 = slab[:, SLAB_W0:SLAB_W0 + 1]
    w1 = slab[:, SLAB_W1:SLAB_W1 + 1]
    x = (x1_ref[...] + w0 * ybuf[0:ROW_DMA_TOKENS, :]
         + w1 * ybuf[ROW_DMA_TOKENS:TOP_K * ROW_DMA_TOKENS, :])
    ms = jnp.mean(x * x, axis=-1, keepdims=True)
    o_ref[...] = x * lax.rsqrt(ms + EPS) * g_ref[...]


def _combine(dest3, x1, slab, g_final, yd):
    n = x1.shape[0]
    return pl.pallas_call(
        _combine_kernel,
        grid=(n // ROW_DMA_TOKENS,),
        in_specs=[
            pl.BlockSpec((1, 1, TOP_K * ROW_DMA_TOKENS), lambda i: (i, 0, 0),
                         memory_space=pltpu.SMEM),
            pl.BlockSpec((ROW_DMA_TOKENS, D_MODEL), lambda i: (i, 0)),
            pl.BlockSpec((ROW_DMA_TOKENS, LANES), lambda i: (i, 0)),
            pl.BlockSpec((1, D_MODEL), lambda i: (0, 0)),
            pl.BlockSpec(memory_space=pl.ANY),
        ],
        out_specs=pl.BlockSpec((ROW_DMA_TOKENS, D_MODEL), lambda i: (i, 0)),
        out_shape=jax.ShapeDtypeStruct((n, D_MODEL), F32),
        scratch_shapes=[pltpu.VMEM((TOP_K * ROW_DMA_TOKENS, D_MODEL), F32),
                        pltpu.SemaphoreType.DMA],
        compiler_params=pltpu.CompilerParams(
            dimension_semantics=("arbitrary",), vmem_limit_bytes=VMEM_LIMIT),
        name="combine_norm",
    )(dest3, x1, slab, g_final, yd)


def _rope_tables(seq):
    pos = jnp.arange(seq, dtype=F32)
    inv_freq = 1.0 / (ROPE_THETA ** (jnp.arange(0, DIFF_QK_DIM, 2, dtype=F32) / DIFF_QK_DIM))
    freqs = pos[:, None] * inv_freq[None, :]
    emb = jnp.concatenate([freqs, freqs], axis=-1)
    cos, sin = jnp.cos(emb), jnp.sin(emb)
    half = DIFF_QK_DIM // 2
    sin_signed = jnp.concatenate([-sin[:, :half], sin[:, half:]], axis=-1)
    reps = LANES // DIFF_QK_DIM
    return jnp.tile(cos, (1, reps)), jnp.tile(sin_signed, (1, reps))


def _split_bf16(w):
    hi = w.astype(BF16)
    lo = (w - hi.astype(F32)).astype(BF16)
    return hi, lo


def kernel(x, g_norm_mix, w_in, lambda_q1, lambda_k1, lambda_q2, lambda_k2, g_subln,
           w_up_a, w_up_b, w_out, g_norm_ffn, w_router_group, b_router_group,
           w_router_expert, b_router_expert, w_expert_gate, w_expert_up, w_expert_down,
           g_norm_final):
    batch, seq, d = x.shape
    depth = w_in.shape[0]
    n_tok = batch * seq
    assert depth == 1, "the final RMSNorm is fused into the layer's combine step"
    assert d == D_MODEL and seq % ATT_BLOCK == 0 and seq % PROJ_ROWS == 0
    assert seq % SB_BLOCK == 0
    assert n_tok % POST_ROWS == 0 and n_tok % ROW_DMA_TOKENS == 0
    n_assign = n_tok * TOP_K
    n_blocks = n_assign // MOE_BLOCK + N_EXPERTS
    n_rows = n_blocks * MOE_BLOCK

    cos_t, sin_t = _rope_tables(seq)
    x2 = x.reshape(n_tok, d)
    for l in range(depth):
        lambda_init = 0.8 - 0.6 * math.exp(-0.3 * l)
        proj = _in_projection(x2, g_norm_mix[l][None, :], w_in[l].astype(BF16),
                              cos_t, sin_t, seq)
        o_a = _sb_attention(proj, batch, seq)
        lam_rows = jnp.stack([lambda_q1[l], lambda_k1[l], lambda_q2[l], lambda_k2[l]])
        o_b = _diff_attention(proj, lam_rows, g_subln[l][None, :], batch, seq, lambda_init)

        w_r = jnp.zeros((d, LANES), F32)
        w_r = w_r.at[:, :N_GROUPS].set(w_router_group[l])
        w_r = w_r.at[:, N_GROUPS:N_GROUPS + N_EXPERTS].set(w_router_expert[l])
        b_r = jnp.zeros((1, LANES), F32)
        b_r = b_r.at[0, :N_GROUPS].set(b_router_group[l])
        b_r = b_r.at[0, N_GROUPS:N_GROUPS + N_EXPERTS].set(b_router_expert[l])
        wr_hi, wr_lo = _split_bf16(w_r)

        x1, h2, slab, cnt = _post_attention(
            o_a, o_b, proj, x2, w_up_a[l].astype(BF16), w_up_b[l].astype(BF16),
            w_out[l].astype(BF16), g_norm_ffn[l][None, :], wr_hi, wr_lo, b_r)

        counts = cnt[0, N_GROUPS:N_GROUPS + N_EXPERTS].astype(jnp.int32)
        padded = (counts + MOE_BLOCK - 1) // MOE_BLOCK * MOE_BLOCK
        pad_end = jnp.cumsum(padded)
        pad_start = pad_end - padded
        e_ids = slab[:, SLAB_E0:SLAB_E1 + 1].astype(jnp.int32)
        ranks = slab[:, SLAB_R0:SLAB_R1 + 1].astype(jnp.int32)
        dest = pad_start[e_ids] + ranks
        dest3 = dest.reshape(n_tok // ROW_DMA_TOKENS, 1, TOP_K * ROW_DMA_TOKENS)
        block_start = jnp.arange(n_blocks, dtype=jnp.int32) * MOE_BLOCK
        block_e = jnp.minimum(
            jnp.sum((pad_end[None, :] <= block_start[:, None]).astype(jnp.int32), axis=1),
            N_EXPERTS - 1)
        n_used = (pad_end[-1:] // MOE_BLOCK).astype(jnp.int32)

        xd = _scatter_rows(dest3, h2, n_rows)
        yd = _expert_mlp(block_e, n_used, xd, w_expert_gate[l].astype(BF16),
                         w_expert_up[l].astype(BF16), w_expert_down[l].astype(BF16))
        x2 = _combine(dest3, x1, slab, g_norm_final[None, :], yd)
    return x2.reshape(batch, seq, d)
```

```python
import functools
import math

import jax
import jax.numpy as jnp
from jax import lax
from jax.experimental import pallas as pl
from jax.experimental.pallas import tpu as pltpu

D_MODEL = 1024
HEAD_DIM = 64
SB_HEADS = 8
SB_WIDTH = SB_HEADS * HEAD_DIM
DIFF_HEADS = 4
DIFF_QK_DIM = 64
DIFF_V_DIM = 2 * DIFF_QK_DIM
DIFF_QK_WIDTH = DIFF_HEADS * 2 * DIFF_QK_DIM
DIFF_V_WIDTH = DIFF_HEADS * DIFF_V_DIM
IN_WIDTH = 3 * SB_WIDTH + 2 * DIFF_QK_WIDTH + DIFF_V_WIDTH + 2 * D_MODEL
ROPE_THETA = 10000.0
N_GROUPS = 4
EXPERTS_PER_GROUP = 8
N_EXPERTS = N_GROUPS * EXPERTS_PER_GROUP
TOP_K = 2
D_EXPERT = 256
EPS = 1e-6

LANES = 128
QK_SCALE = HEAD_DIM ** -0.5

F32 = jnp.float32
BF16 = jnp.bfloat16

PROJ_ROWS = 512
PROJ_CHUNK = 512
SB_BLOCK = 256
ATT_BLOCK = 512
SB_LOG_UNDERFLOW = -105.0
POST_ROWS = 512
MOE_BLOCK = 256
ROW_DMA_TOKENS = 256
ROW_DMA_UNROLL = 8
VMEM_LIMIT = 56 * 1024 * 1024

SBQ_CB, SBK_CB, SBV_CB = 0, 4, 8
DQ_CB, DK_CB, DV_CB = 12, 16, 20
GATE_A_B1024, GATE_B_B1024 = 3, 4

SLAB_E0, SLAB_E1, SLAB_R0, SLAB_R1, SLAB_W0, SLAB_W1 = 0, 1, 2, 3, 4, 5
ROUTER_EXPERT_LANE0 = N_GROUPS


def _nt_dot(a, b):
    return lax.dot_general(a, b, (((1,), (1,)), ((), ())), preferred_element_type=F32)


ROW_TILE = D_MODEL // LANES


def _store_row_tiles(ref, value, first_row=0):
    rows = value.shape[0]
    for k in range(ROW_TILE):
        ref[pl.ds(first_row * ROW_TILE + k, rows, stride=ROW_TILE), :] = (
            value[:, k * LANES:(k + 1) * LANES])


def _load_row_tiles(ref, rows, first_row=0):
    return jnp.concatenate(
        [ref[pl.ds(first_row * ROW_TILE + k, rows, stride=ROW_TILE), :]
         for k in range(ROW_TILE)], axis=1)


def _row_tile_copy(src_ref, src_row, dst_ref, dst_row, sem):
    src = src_ref.at[pl.ds(pl.multiple_of(src_row * ROW_TILE, ROW_TILE), ROW_TILE), :]
    dst = dst_ref.at[pl.ds(pl.multiple_of(dst_row * ROW_TILE, ROW_TILE), ROW_TILE), :]
    return pltpu.make_async_copy(src, dst, sem)


def _inproj_kernel(x_ref, g_ref, w_ref, cos_ref, sin_ref, o_ref):
    x = x_ref[...]
    ms = jnp.mean(x * x, axis=-1, keepdims=True)
    h = (x * lax.rsqrt(ms + EPS) * g_ref[...]).astype(BF16)
    lane = lax.broadcasted_iota(jnp.int32, (PROJ_ROWS, LANES), 1)
    first_half = (lane % DIFF_QK_DIM) < (DIFF_QK_DIM // 2)
    cos = cos_ref[...]
    sin = sin_ref[...]

    def rope(acc):
        outs = []
        for s in range(PROJ_CHUNK // LANES):
            a = acc[:, s * LANES:(s + 1) * LANES]
            swapped = jnp.where(first_half,
                                pltpu.roll(a, LANES - DIFF_QK_DIM // 2, 1),
                                pltpu.roll(a, DIFF_QK_DIM // 2, 1))
            outs.append(a * cos + swapped * sin)
        return jnp.concatenate(outs, axis=1)

    for c in range(IN_WIDTH // PROJ_CHUNK):
        lo = c * PROJ_CHUNK
        acc = jnp.dot(h, w_ref[:, lo:lo + PROJ_CHUNK], preferred_element_type=F32)
        if lo == 0:
            acc = acc * QK_SCALE
        elif lo == DQ_CB * LANES:
            acc = rope(acc) * QK_SCALE
        elif lo == DK_CB * LANES:
            acc = rope(acc)
        o_ref[:, lo:lo + PROJ_CHUNK] = acc.astype(BF16)


def _in_projection(x2, g, w_bf, cos_t, sin_t, seq):
    n = x2.shape[0]
    pos_blocks = seq // PROJ_ROWS
    return pl.pallas_call(
        _inproj_kernel,
        grid=(n // PROJ_ROWS,),
        in_specs=[
            pl.BlockSpec((PROJ_ROWS, D_MODEL), lambda i: (i, 0)),
            pl.BlockSpec((1, D_MODEL), lambda i: (0, 0)),
            pl.BlockSpec((D_MODEL, IN_WIDTH), lambda i: (0, 0)),
            pl.BlockSpec((PROJ_ROWS, LANES), lambda i: (i % pos_blocks, 0)),
            pl.BlockSpec((PROJ_ROWS, LANES), lambda i: (i % pos_blocks, 0)),
        ],
        out_specs=pl.BlockSpec((PROJ_ROWS, IN_WIDTH), lambda i: (i, 0)),
        out_shape=jax.ShapeDtypeStruct((n, IN_WIDTH), BF16),
        compiler_params=pltpu.CompilerParams(
            dimension_semantics=("parallel",), vmem_limit_bytes=VMEM_LIMIT),
        name="in_projection",
    )(x2, g, w_bf, cos_t, sin_t)


def _sb_kernel(q_ref, k_ref, v_ref, o_ref):
    i = pl.program_id(2)
    q = q_ref[...]
    lane = lax.broadcasted_iota(jnp.int32, (SB_BLOCK, LANES), 1)
    row = lax.broadcasted_iota(jnp.int32, (SB_BLOCK, SB_BLOCK), 0)
    col = lax.broadcasted_iota(jnp.int32, (SB_BLOCK, SB_BLOCK), 1)
    strict = col < row
    suffix = jnp.where(row >= col, 1.0, 0.0).astype(BF16)
    zero = jnp.zeros_like(q)
    qs = (jnp.where(lane < HEAD_DIM, q, zero), jnp.where(lane >= HEAD_DIM, q, zero))

    def block(j, qh, run, masked):
        start = pl.multiple_of(j * SB_BLOCK, SB_BLOCK)
        z = _nt_dot(qh, k_ref[pl.ds(start, SB_BLOCK), :])
        l = -(jnp.maximum(z, 0.0) + jnp.log(1.0 + jnp.exp(-jnp.abs(z))))
        if masked:
            l = jnp.where(strict, l, 0.0)
        l_hi = l.astype(BF16)
        l_lo = (l - l_hi.astype(F32)).astype(BF16)
        inc = (jnp.dot(l_hi, suffix, preferred_element_type=F32)
               + jnp.dot(l_lo, suffix, preferred_element_type=F32)) + run
        a = jnp.exp(z + inc)
        if masked:
            a = jnp.where(strict, a, 0.0)
        return a, inc[:, 0:1], start

    def weighted_values(a, start):
        return jnp.dot(a.astype(BF16), v_ref[pl.ds(start, SB_BLOCK), :],
                       preferred_element_type=F32)

    has_prev = i >= 1
    prev = jnp.maximum(i - 1, 0)
    runs, accs = [], []
    for h in range(2):
        a_d, run_d, s_d = block(i, qs[h], jnp.zeros((SB_BLOCK, 1), F32), True)
        a_p, run_p, s_p = block(prev, qs[h], run_d, False)
        a_p = jnp.where(has_prev, a_p, 0.0)
        accs.append(weighted_values(a_d, s_d) + weighted_values(a_p, s_p))
        runs.append(run_p)

    def alive(r0, r1):
        return (jnp.max(jnp.maximum(r0, r1)) > SB_LOG_UNDERFLOW).astype(jnp.int32)

    def cond(c):
        return (c[0] >= 0) & (c[1] > 0)

    def body(c):
        j = c[0]
        new_runs, new_accs = [], []
        for h in range(2):
            a, run, start = block(j, qs[h], c[2 + h], False)
            new_accs.append(c[4 + h] + weighted_values(a, start))
            new_runs.append(run)
        return (j - 1, alive(*new_runs), *new_runs, *new_accs)

    out = lax.while_loop(cond, body, (i - 2, alive(*runs), *runs, *accs))
    o_ref[...] = jnp.where(lane < HEAD_DIM, out[4], out[5]).astype(o_ref.dtype)


def _sb_attention(proj, batch, seq):
    nq = seq // SB_BLOCK
    return pl.pallas_call(
        _sb_kernel,
        grid=(batch, SB_WIDTH // LANES, nq),
        in_specs=[
            pl.BlockSpec((SB_BLOCK, LANES), lambda b, p, i: (b * nq + i, SBQ_CB + p)),
            pl.BlockSpec((seq, LANES), lambda b, p, i: (b, SBK_CB + p)),
            pl.BlockSpec((seq, LANES), lambda b, p, i: (b, SBV_CB + p)),
        ],
        out_specs=pl.BlockSpec((SB_BLOCK, LANES), lambda b, p, i: (b * nq + i, p)),
        out_shape=jax.ShapeDtypeStruct((batch * seq, SB_WIDTH), BF16),
        compiler_params=pltpu.CompilerParams(
            dimension_semantics=("parallel", "parallel", "arbitrary"),
            vmem_limit_bytes=VMEM_LIMIT),
        name="stickbreak_attention",
    )(proj, proj, proj)


def _diff_kernel(lam_ref, gs_ref, q_ref, k_ref, v_ref, o_ref, *, lambda_init):
    i = pl.program_id(2)
    q = q_ref[...]
    lane = lax.broadcasted_iota(jnp.int32, (ATT_BLOCK, LANES), 1)
    row = lax.broadcasted_iota(jnp.int32, (ATT_BLOCK, ATT_BLOCK), 0)
    col = lax.broadcasted_iota(jnp.int32, (ATT_BLOCK, ATT_BLOCK), 1)
    causal = col <= row
    zero = jnp.zeros_like(q)
    qs = (jnp.where(lane < DIFF_QK_DIM, q, zero), jnp.where(lane >= DIFF_QK_DIM, q, zero))

    lv = lam_ref[...]
    lam = (jnp.exp(jnp.sum(lv[0:1] * lv[1:2], axis=-1, keepdims=True))
           - jnp.exp(jnp.sum(lv[2:3] * lv[3:4], axis=-1, keepdims=True)) + lambda_init)

    def step(j, carry, masked):
        start = pl.multiple_of(j * ATT_BLOCK, ATT_BLOCK)
        kb = k_ref[pl.ds(start, ATT_BLOCK), :]
        vb = v_ref[pl.ds(start, ATT_BLOCK), :]
        out = []
        for c in range(2):
            m, l, acc = carry[c]
            s = _nt_dot(qs[c], kb)
            if masked:
                s = jnp.where(causal, s, -jnp.inf)
            m_new = jnp.maximum(m, jnp.max(s, axis=-1, keepdims=True))
            alpha = jnp.exp(m - m_new)
            p = jnp.exp(s - m_new)
            l = alpha * l + jnp.sum(p, axis=-1, keepdims=True)
            acc = alpha * acc + jnp.dot(p.astype(BF16), vb, preferred_element_type=F32)
            out.append((m_new, l, acc))
        return tuple(out)

    init = tuple((jnp.full((ATT_BLOCK, 1), -jnp.inf, F32), jnp.zeros((ATT_BLOCK, 1), F32),
                  jnp.zeros((ATT_BLOCK, LANES), F32)) for _ in range(2))
    carry = step(i, init, True)
    carry = lax.fori_loop(0, i, lambda j, c: step(j, c, False), carry)
    (_, l1, a1), (_, l2, a2) = carry
    o = a1 / l1 - lam * (a2 / l2)
    ms = jnp.mean(o * o, axis=-1, keepdims=True)
    o = o * lax.rsqrt(ms + EPS) * gs_ref[...] * (1.0 - lambda_init)
    o_ref[...] = o.astype(o_ref.dtype)


def _diff_attention(proj, lam_rows, g_subln, batch, seq, lambda_init):
    nq = seq // ATT_BLOCK
    return pl.pallas_call(
        functools.partial(_diff_kernel, lambda_init=lambda_init),
        grid=(batch, DIFF_HEADS, nq),
        in_specs=[
            pl.BlockSpec((4, DIFF_QK_DIM), lambda b, h, i: (0, 0)),
            pl.BlockSpec((1, DIFF_V_DIM), lambda b, h, i: (0, 0)),
            pl.BlockSpec((ATT_BLOCK, LANES), lambda b, h, i: (b * nq + i, DQ_CB + h)),
            pl.BlockSpec((seq, LANES), lambda b, h, i: (b, DK_CB + h)),
            pl.BlockSpec((seq, LANES), lambda b, h, i: (b, DV_CB + h)),
        ],
        out_specs=pl.BlockSpec((ATT_BLOCK, LANES), lambda b, h, i: (b * nq + i, h)),
        out_shape=jax.ShapeDtypeStruct((batch * seq, DIFF_V_WIDTH), BF16),
        compiler_params=pltpu.CompilerParams(
            dimension_semantics=("parallel", "parallel", "arbitrary"),
            vmem_limit_bytes=VMEM_LIMIT),
        name="differential_attention",
    )(lam_rows, g_subln, proj, proj, proj)


def _post_kernel(oa_ref, ob_ref, ga_ref, gb_ref, x_ref, wua_ref, wub_ref, wo_ref, gn_ref,
                 wr_hi_ref, wr_lo_ref, br_ref,
                 x1_ref, h2_ref, slab_ref, cnt_ref, carry_ref):
    step_id = pl.program_id(0)

    @pl.when(step_id == 0)
    def _():
        carry_ref[...] = jnp.zeros_like(carry_ref)

    u_a = jnp.dot(oa_ref[...], wua_ref[...], preferred_element_type=F32)
    u_b = jnp.dot(ob_ref[...], wub_ref[...], preferred_element_type=F32)
    y = (jax.nn.sigmoid(ga_ref[...].astype(F32)) * u_a
         + jax.nn.sigmoid(gb_ref[...].astype(F32)) * u_b)
    x1 = x_ref[...] + jnp.dot(y.astype(BF16), wo_ref[...], preferred_element_type=F32)
    x1_ref[...] = x1
    ms = jnp.mean(x1 * x1, axis=-1, keepdims=True)
    h2 = x1 * lax.rsqrt(ms + EPS) * gn_ref[...]
    _store_row_tiles(h2_ref, h2)

    h_hi = h2.astype(BF16)
    h_lo = (h2 - h_hi.astype(F32)).astype(BF16)
    logits = (jnp.dot(h_hi, wr_hi_ref[...], preferred_element_type=F32)
              + jnp.dot(h_lo, wr_hi_ref[...], preferred_element_type=F32)
              + jnp.dot(h_hi, wr_lo_ref[...], preferred_element_type=F32)
              + br_ref[...])

    lane = lax.broadcasted_iota(jnp.int32, (POST_ROWS, LANES), 1)
    neg = -jnp.inf

    def first_argmax(v):
        m = jnp.max(v, axis=-1, keepdims=True)
        idx = jnp.min(jnp.where(v == m, lane, LANES), axis=-1, keepdims=True)
        return m, idx

    g_logits = jnp.where(lane < N_GROUPS, logits, neg)
    g_max, grp = first_argmax(g_logits)
    p_grp = 1.0 / jnp.sum(jnp.exp(g_logits - g_max), axis=-1, keepdims=True)

    lo_lane = ROUTER_EXPERT_LANE0 + grp * EXPERTS_PER_GROUP
    in_grp = (lane >= lo_lane) & (lane < lo_lane + EXPERTS_PER_GROUP)
    e_logits = jnp.where(in_grp, logits, neg)
    m1, i1 = first_argmax(e_logits)
    m2, i2 = first_argmax(jnp.where(lane == i1, neg, e_logits))
    r = jnp.exp(m2 - m1)
    w0 = p_grp / (1.0 + r)
    w1 = p_grp * r / (1.0 + r)

    hit0 = lane == i1
    hit1 = lane == i2
    onehot = jnp.where(hit0 | hit1, 1.0, 0.0)
    trow = lax.broadcasted_iota(jnp.int32, (POST_ROWS, POST_ROWS), 0)
    tcol = lax.broadcasted_iota(jnp.int32, (POST_ROWS, POST_ROWS), 1)
    earlier = jnp.where(tcol < trow, 1.0, 0.0).astype(BF16)
    before = carry_ref[0:1, :] + jnp.dot(earlier, onehot.astype(BF16),
                                         preferred_element_type=F32)
    r0 = jnp.sum(jnp.where(hit0, before, 0.0), axis=-1, keepdims=True)
    r1 = jnp.sum(jnp.where(hit1, before, 0.0), axis=-1, keepdims=True)
    new_carry = carry_ref[0:1, :] + jnp.sum(onehot, axis=0, keepdims=True)
    carry_ref[0:1, :] = new_carry
    cnt_ref[...] = jnp.broadcast_to(new_carry, cnt_ref.shape)

    e0 = (i1 - ROUTER_EXPERT_LANE0).astype(F32)
    e1 = (i2 - ROUTER_EXPERT_LANE0).astype(F32)
    slab = jnp.zeros((POST_ROWS, LANES), F32)
    for pos, val in ((SLAB_E0, e0), (SLAB_E1, e1), (SLAB_R0, r0), (SLAB_R1, r1),
                     (SLAB_W0, w0), (SLAB_W1, w1)):
        slab = jnp.where(lane == pos, val, slab)
    slab_ref[...] = slab


def _post_attention(o_a, o_b, proj, x2, wua, wub, wo, gn, wr_hi, wr_lo, br):
    n = x2.shape[0]
    const = lambda i: (0, 0)
    return pl.pallas_call(
        _post_kernel,
        grid=(n // POST_ROWS,),
        in_specs=[
            pl.BlockSpec((POST_ROWS, SB_WIDTH), lambda i: (i, 0)),
            pl.BlockSpec((POST_ROWS, DIFF_V_WIDTH), lambda i: (i, 0)),
            pl.BlockSpec((POST_ROWS, D_MODEL), lambda i: (i, GATE_A_B1024)),
            pl.BlockSpec((POST_ROWS, D_MODEL), lambda i: (i, GATE_B_B1024)),
            pl.BlockSpec((POST_ROWS, D_MODEL), lambda i: (i, 0)),
            pl.BlockSpec((SB_WIDTH, D_MODEL), const),
            pl.BlockSpec((DIFF_V_WIDTH, D_MODEL), const),
            pl.BlockSpec((D_MODEL, D_MODEL), const),
            pl.BlockSpec((1, D_MODEL), const),
            pl.BlockSpec((D_MODEL, LANES), const),
            pl.BlockSpec((D_MODEL, LANES), const),
            pl.BlockSpec((1, LANES), const),
        ],
        out_specs=[
            pl.BlockSpec((POST_ROWS, D_MODEL), lambda i: (i, 0)),
            pl.BlockSpec((POST_ROWS * ROW_TILE, LANES), lambda i: (i, 0)),
            pl.BlockSpec((POST_ROWS, LANES), lambda i: (i, 0)),
            pl.BlockSpec((8, LANES), const),
        ],
        out_shape=[
            jax.ShapeDtypeStruct((n, D_MODEL), F32),
            jax.ShapeDtypeStruct((n * ROW_TILE, LANES), F32),
            jax.ShapeDtypeStruct((n, LANES), F32),
            jax.ShapeDtypeStruct((8, LANES), F32),
        ],
        scratch_shapes=[pltpu.VMEM((8, LANES), F32)],
        compiler_params=pltpu.CompilerParams(
            dimension_semantics=("arbitrary",), vmem_limit_bytes=VMEM_LIMIT),
        name="merge_route",
    )(o_a, o_b, proj, proj, x2, wua, wub, wo, gn, wr_hi, wr_lo, br)


def _issue_row_copies(make_copy):
    def group(g, _):
        for u in range(ROW_DMA_UNROLL):
            for k in range(TOP_K):
                make_copy(g * ROW_DMA_UNROLL + u, k).start(priority=k % 2)
        return 0

    lax.fori_loop(0, ROW_DMA_TOKENS // ROW_DMA_UNROLL, group, 0)


def _scatter_kernel(dest_ref, h_ref, zeros_ref, xd_ref, sem):
    del zeros_ref
    _issue_row_copies(
        lambda t, k: _row_tile_copy(h_ref, t, xd_ref, dest_ref[0, 0, t * TOP_K + k], sem))
    for _ in range(TOP_K):
        pltpu.make_async_copy(
            h_ref, xd_ref.at[pl.ds(0, ROW_DMA_TOKENS * ROW_TILE), :], sem).wait()


def _scatter_rows(dest3, h2_tiles, n_rows):
    n = h2_tiles.shape[0] // ROW_TILE
    return pl.pallas_call(
        _scatter_kernel,
        grid=(n // ROW_DMA_TOKENS,),
        in_specs=[
            pl.BlockSpec((1, 1, TOP_K * ROW_DMA_TOKENS), lambda i: (i, 0, 0),
                         memory_space=pltpu.SMEM),
            pl.BlockSpec((ROW_DMA_TOKENS * ROW_TILE, LANES), lambda i: (i, 0)),
            pl.BlockSpec(memory_space=pl.ANY),
        ],
        out_specs=pl.BlockSpec(memory_space=pl.ANY),
        out_shape=jax.ShapeDtypeStruct((n_rows * ROW_TILE, LANES), F32),
        input_output_aliases={2: 0},
        scratch_shapes=[pltpu.SemaphoreType.DMA],
        compiler_params=pltpu.CompilerParams(
            dimension_semantics=("arbitrary",), vmem_limit_bytes=VMEM_LIMIT),
        name="scatter_rows",
    )(dest3, h2_tiles, jnp.zeros((n_rows * ROW_TILE, LANES), F32))


def _expert_kernel(be_ref, nb_ref, x_ref, wg_ref, wu_ref, wd_ref, y_ref):
    @pl.when(pl.program_id(0) < nb_ref[0])
    def _():
        x = _load_row_tiles(x_ref, MOE_BLOCK).astype(BF16)
        g = jnp.dot(x, wg_ref[0], preferred_element_type=F32)
        u = jnp.dot(x, wu_ref[0], preferred_element_type=F32)
        hmid = (g * jax.nn.sigmoid(g) * u).astype(BF16)
        _store_row_tiles(y_ref, jnp.dot(hmid, wd_ref[0], preferred_element_type=F32))

    @pl.when(pl.program_id(0) >= nb_ref[0])
    def _():
        y_ref[...] = jnp.zeros_like(y_ref)


def _expert_mlp(block_e, n_used, xd_tiles, wg, wu, wd):
    n_blocks = xd_tiles.shape[0] // (MOE_BLOCK * ROW_TILE)

    def row_map(i, be, nb):
        return (jnp.minimum(i, nb[0] - 1), 0)

    def w_map(i, be, nb):
        return (be[i], 0, 0)

    return pl.pallas_call(
        _expert_kernel,
        grid_spec=pltpu.PrefetchScalarGridSpec(
            num_scalar_prefetch=2,
            grid=(n_blocks,),
            in_specs=[
                pl.BlockSpec((MOE_BLOCK * ROW_TILE, LANES), row_map),
                pl.BlockSpec((1, D_MODEL, D_EXPERT), w_map),
                pl.BlockSpec((1, D_MODEL, D_EXPERT), w_map),
                pl.BlockSpec((1, D_EXPERT, D_MODEL), w_map),
            ],
            out_specs=pl.BlockSpec((MOE_BLOCK * ROW_TILE, LANES), lambda i, be, nb: (i, 0)),
        ),
        out_shape=jax.ShapeDtypeStruct(xd_tiles.shape, F32),
        compiler_params=pltpu.CompilerParams(
            dimension_semantics=("arbitrary",), vmem_limit_bytes=VMEM_LIMIT),
        name="expert_mlp",
    )(block_e, n_used, xd_tiles, wg, wu, wd)


def _combine_kernel(dest_ref, x1_ref, slab_ref, g_ref, yd_ref, o_ref, ybuf, sem):
    n_copies = TOP_K * ROW_DMA_TOKENS
    _issue_row_copies(
        lambda t, k: _row_tile_copy(yd_ref, dest_ref[0, 0, t * TOP_K + k], ybuf,
                                    k * ROW_DMA_TOKENS + t, sem))
    pltpu.make_async_copy(yd_ref.at[pl.ds(0, n_copies * ROW_TILE), :], ybuf, sem).wait()

    slab = slab_ref[...]
    w0 = slab[:, SLAB_W0:SLAB_W0 + 1]
    w1 = slab[:, SLAB_W1:SLAB_W1 + 1]
    x = (x1_ref[...] + w0 * _load_row_tiles(ybuf, ROW_DMA_TOKENS)
         + w1 * _load_row_tiles(ybuf, ROW_DMA_TOKENS, first_row=ROW_DMA_TOKENS))
    ms = jnp.mean(x * x, axis=-1, keepdims=True)
    o_ref[...] = x * lax.rsqrt(ms + EPS) * g_ref[...]


def _combine(dest3, x1, slab, g_final, yd_tiles):
    n = x1.shape[0]
    return pl.pallas_call(
        _combine_kernel,
        grid=(n // ROW_DMA_TOKENS,),
        in_specs=[
            pl.BlockSpec((1, 1, TOP_K * ROW_DMA_TOKENS), lambda i: (i, 0, 0),
                         memory_space=pltpu.SMEM),
            pl.BlockSpec((ROW_DMA_TOKENS, D_MODEL), lambda i: (i, 0)),
            pl.BlockSpec((ROW_DMA_TOKENS, LANES), lambda i: (i, 0)),
            pl.BlockSpec((1, D_MODEL), lambda i: (0, 0)),
            pl.BlockSpec(memory_space=pl.ANY),
        ],
        out_specs=pl.BlockSpec((ROW_DMA_TOKENS, D_MODEL), lambda i: (i, 0)),
        out_shape=jax.ShapeDtypeStruct((n, D_MODEL), F32),
        scratch_shapes=[pltpu.VMEM((TOP_K * ROW_DMA_TOKENS * ROW_TILE, LANES), F32),
                        pltpu.SemaphoreType.DMA],
        compiler_params=pltpu.CompilerParams(
            dimension_semantics=("arbitrary",), vmem_limit_bytes=VMEM_LIMIT),
        name="combine_norm",
    )(dest3, x1, slab, g_final, yd_tiles)


def _rope_tables(seq):
    pos = jnp.arange(seq, dtype=F32)
    inv_freq = 1.0 / (ROPE_THETA ** (jnp.arange(0, DIFF_QK_DIM, 2, dtype=F32) / DIFF_QK_DIM))
    freqs = pos[:, None] * inv_freq[None, :]
    emb = jnp.concatenate([freqs, freqs], axis=-1)
    cos, sin = jnp.cos(emb), jnp.sin(emb)
    half = DIFF_QK_DIM // 2
    sin_signed = jnp.concatenate([-sin[:, :half], sin[:, half:]], axis=-1)
    reps = LANES // DIFF_QK_DIM
    return jnp.tile(cos, (1, reps)), jnp.tile(sin_signed, (1, reps))


def _split_bf16(w):
    hi = w.astype(BF16)
    lo = (w - hi.astype(F32)).astype(BF16)
    return hi, lo


def kernel(x, g_norm_mix, w_in, lambda_q1, lambda_k1, lambda_q2, lambda_k2, g_subln,
           w_up_a, w_up_b, w_out, g_norm_ffn, w_router_group, b_router_group,
           w_router_expert, b_router_expert, w_expert_gate, w_expert_up, w_expert_down,
           g_norm_final):
    batch, seq, d = x.shape
    depth = w_in.shape[0]
    n_tok = batch * seq
    assert depth == 1, "the final RMSNorm is fused into the layer's combine step"
    assert d == D_MODEL and seq % ATT_BLOCK == 0 and seq % PROJ_ROWS == 0
    assert seq % SB_BLOCK == 0
    assert n_tok % POST_ROWS == 0 and n_tok % ROW_DMA_TOKENS == 0
    n_assign = n_tok * TOP_K
    n_blocks = n_assign // MOE_BLOCK + N_EXPERTS
    n_rows = n_blocks * MOE_BLOCK

    cos_t, sin_t = _rope_tables(seq)
    x2 = x.reshape(n_tok, d)
    for l in range(depth):
        lambda_init = 0.8 - 0.6 * math.exp(-0.3 * l)
        proj = _in_projection(x2, g_norm_mix[l][None, :], w_in[l].astype(BF16),
                              cos_t, sin_t, seq)
        o_a = _sb_attention(proj, batch, seq)
        lam_rows = jnp.stack([lambda_q1[l], lambda_k1[l], lambda_q2[l], lambda_k2[l]])
        o_b = _diff_attention(proj, lam_rows, g_subln[l][None, :], batch, seq, lambda_init)

        w_r = jnp.zeros((d, LANES), F32)
        w_r = w_r.at[:, :N_GROUPS].set(w_router_group[l])
        w_r = w_r.at[:, N_GROUPS:N_GROUPS + N_EXPERTS].set(w_router_expert[l])
        b_r = jnp.zeros((1, LANES), F32)
        b_r = b_r.at[0, :N_GROUPS].set(b_router_group[l])
        b_r = b_r.at[0, N_GROUPS:N_GROUPS + N_EXPERTS].set(b_router_expert[l])
        wr_hi, wr_lo = _split_bf16(w_r)

        x1, h2, slab, cnt = _post_attention(
            o_a, o_b, proj, x2, w_up_a[l].astype(BF16), w_up_b[l].astype(BF16),
            w_out[l].astype(BF16), g_norm_ffn[l][None, :], wr_hi, wr_lo, b_r)

        counts = cnt[0, N_GROUPS:N_GROUPS + N_EXPERTS].astype(jnp.int32)
        padded = (counts + MOE_BLOCK - 1) // MOE_BLOCK * MOE_BLOCK
        pad_end = jnp.cumsum(padded)
        pad_start = pad_end - padded
        e_ids = slab[:, SLAB_E0:SLAB_E1 + 1].astype(jnp.int32)
        ranks = slab[:, SLAB_R0:SLAB_R1 + 1].astype(jnp.int32)
        dest = pad_start[e_ids] + ranks
        dest3 = dest.reshape(n_tok // ROW_DMA_TOKENS, 1, TOP_K * ROW_DMA_TOKENS)
        block_start = jnp.arange(n_blocks, dtype=jnp.int32) * MOE_BLOCK
        block_e = jnp.minimum(
            jnp.sum((pad_end[None, :] <= block_start[:, None]).astype(jnp.int32), axis=1),
            N_EXPERTS - 1)
        n_used = (pad_end[-1:] // MOE_BLOCK).astype(jnp.int32)

        xd = _scatter_rows(dest3, h2, n_rows)
        yd = _expert_mlp(block_e, n_used, xd, w_expert_gate[l].astype(BF16),
                         w_expert_up[l].astype(BF16), w_expert_down[l].astype(BF16))
        x2 = _combine(dest3, x1, slab, g_norm_final[None, :], yd)
    return x2.reshape(batch, seq, d)
```

```python
import functools
import math

import jax
import jax.numpy as jnp
from jax import lax
from jax.experimental import pallas as pl
from jax.experimental.pallas import tpu as pltpu

D_MODEL = 1024
HEAD_DIM = 64
SB_HEADS = 8
SB_WIDTH = SB_HEADS * HEAD_DIM
DIFF_HEADS = 4
DIFF_QK_DIM = 64
DIFF_V_DIM = 2 * DIFF_QK_DIM
DIFF_QK_WIDTH = DIFF_HEADS * 2 * DIFF_QK_DIM
DIFF_V_WIDTH = DIFF_HEADS * DIFF_V_DIM
IN_WIDTH = 3 * SB_WIDTH + 2 * DIFF_QK_WIDTH + DIFF_V_WIDTH + 2 * D_MODEL
ROPE_THETA = 10000.0
N_GROUPS = 4
EXPERTS_PER_GROUP = 8
N_EXPERTS = N_GROUPS * EXPERTS_PER_GROUP
TOP_K = 2
D_EXPERT = 256
EPS = 1e-6

LANES = 128
QK_SCALE = HEAD_DIM ** -0.5
LOG2E = math.log2(math.e)

F32 = jnp.float32
BF16 = jnp.bfloat16

PROJ_ROWS = 512
PROJ_CHUNK = 512
SB_BLOCK = 256
DIFF_BLOCK = 512
SB_MIN_WEIGHT = 2.0 ** -126
SB_MAX_LOG2 = 126.0
POST_ROWS = 512
MOE_BLOCK = 256
MOE_GROUP = 4
ROW_DMA_TOKENS = 512
ROW_DMA_UNROLL = 8
VMEM_LIMIT = 56 * 1024 * 1024

SB_RUN = SB_BLOCK // 8
ROW_WIDTH = SB_WIDTH + DIFF_QK_WIDTH + 2 * D_MODEL
T_WIDTH = DIFF_QK_WIDTH + DIFF_V_WIDTH
TS_WIDTH = 2 * SB_WIDTH
SBK_CB, DK_CB = 0, 4
GATE_A_B1024, GATE_B_B1024 = 1, 2
DQ_RB, DV_RB = 0, DIFF_QK_WIDTH // LANES
SBQ_RB, SBV_RB = 0, SB_WIDTH // LANES

SLAB_E0, SLAB_E1, SLAB_R0, SLAB_R1, SLAB_W0, SLAB_W1 = 0, 1, 2, 3, 4, 5
ROUTER_EXPERT_LANE0 = N_GROUPS


def _nt_dot(a, b):
    return lax.dot_general(a, b, (((1,), (1,)), ((), ())), preferred_element_type=F32)


ROW_TILE = D_MODEL // LANES


def _store_row_tiles(ref, value, first_row=0):
    rows = value.shape[0]
    for k in range(ROW_TILE):
        ref[pl.ds(first_row * ROW_TILE + k, rows, stride=ROW_TILE), :] = (
            value[:, k * LANES:(k + 1) * LANES])


def _load_row_tiles(ref, rows, first_row=0):
    return jnp.concatenate(
        [ref[pl.ds(first_row * ROW_TILE + k, rows, stride=ROW_TILE), :]
         for k in range(ROW_TILE)], axis=1)


def _row_tile_copy(src_ref, src_row, dst_ref, dst_row, sem):
    src = src_ref.at[pl.ds(pl.multiple_of(src_row * ROW_TILE, ROW_TILE), ROW_TILE), :]
    dst = dst_ref.at[pl.ds(pl.multiple_of(dst_row * ROW_TILE, ROW_TILE), ROW_TILE), :]
    return pltpu.make_async_copy(src, dst, sem)


def _inproj_kernel(x_ref, g_ref, w_ref, wt_ref, wts_ref, cos_ref, sin_ref, cost_ref, sint_ref,
                   o_ref, ot_ref, ots_ref):
    def normed(x):
        ms = jnp.mean(x * x, axis=-1, keepdims=True)
        return (x * lax.rsqrt(ms + EPS) * g_ref[...]).astype(BF16)

    h = normed(x_ref[...])
    dst = lax.broadcasted_iota(jnp.int32, (PROJ_ROWS, PROJ_ROWS), 0)
    src = lax.broadcasted_iota(jnp.int32, (PROJ_ROWS, PROJ_ROWS), 1)
    in_block = dst % SB_BLOCK
    src_of_dst = (dst - in_block) + (in_block % 8) * SB_RUN + in_block // 8
    perm = jnp.where(src == src_of_dst, 1.0, 0.0).astype(BF16)
    h_perm = jnp.dot(perm, h, preferred_element_type=F32).astype(BF16)

    lane = lax.broadcasted_iota(jnp.int32, (PROJ_ROWS, LANES), 1)
    first_half = (lane % DIFF_QK_DIM) < (DIFF_QK_DIM // 2)
    cos = cos_ref[...]
    sin = sin_ref[...]

    def rope(acc):
        outs = []
        for s in range(PROJ_CHUNK // LANES):
            a = acc[:, s * LANES:(s + 1) * LANES]
            swapped = jnp.where(first_half,
                                pltpu.roll(a, LANES - DIFF_QK_DIM // 2, 1),
                                pltpu.roll(a, DIFF_QK_DIM // 2, 1))
            outs.append(a * cos + swapped * sin)
        return jnp.concatenate(outs, axis=1)

    for c in range(ROW_WIDTH // PROJ_CHUNK):
        lo = c * PROJ_CHUNK
        lhs = h_perm if lo == SBK_CB * LANES else h
        acc = jnp.dot(lhs, w_ref[:, lo:lo + PROJ_CHUNK], preferred_element_type=F32)
        if lo == DK_CB * LANES:
            acc = rope(acc)
        o_ref[:, lo:lo + PROJ_CHUNK] = acc.astype(BF16)

    qt = _nt_dot(wt_ref[0:DIFF_QK_WIDTH, :], h)
    cos_f = cost_ref[...]
    sin_f = sint_ref[...]
    half = DIFF_QK_DIM // 2
    groups = []
    for gidx in range(DIFF_QK_WIDTH // DIFF_QK_DIM):
        xg = qt[gidx * DIFF_QK_DIM:(gidx + 1) * DIFF_QK_DIM]
        swapped = jnp.concatenate([xg[half:], xg[:half]], axis=0)
        groups.append((xg * cos_f + swapped * sin_f) * (QK_SCALE * LOG2E))
    ot_ref[0, 0:DIFF_QK_WIDTH, :] = jnp.concatenate(groups, axis=0).astype(BF16)
    ot_ref[0, DIFF_QK_WIDTH:T_WIDTH, :] = _nt_dot(wt_ref[DIFF_QK_WIDTH:T_WIDTH, :], h).astype(BF16)

    sq = (_nt_dot(wts_ref[0:SB_WIDTH, :], h) * (QK_SCALE * LOG2E)).astype(BF16)
    sv = _nt_dot(wts_ref[SB_WIDTH:TS_WIDTH, :], h_perm).astype(BF16)
    for blk in range(PROJ_ROWS // SB_BLOCK):
        cols = slice(blk * SB_BLOCK, (blk + 1) * SB_BLOCK)
        ots_ref[blk, 0:SB_WIDTH, :] = sq[:, cols]
        ots_ref[blk, SB_WIDTH:TS_WIDTH, :] = sv[:, cols]


def _in_projection(x2, g, w_rows, w_feat, w_feat_sb, rope, seq):
    n = x2.shape[0]
    pos_blocks = seq // PROJ_ROWS
    sb_per_step = PROJ_ROWS // SB_BLOCK
    cos_t, sin_t, cos_f, sin_f = rope
    const = lambda i: (0, 0)
    return pl.pallas_call(
        _inproj_kernel,
        grid=(n // PROJ_ROWS,),
        in_specs=[
            pl.BlockSpec((PROJ_ROWS, D_MODEL), lambda i: (i, 0)),
            pl.BlockSpec((1, D_MODEL), const),
            pl.BlockSpec((D_MODEL, ROW_WIDTH), const),
            pl.BlockSpec((T_WIDTH, D_MODEL), const),
            pl.BlockSpec((TS_WIDTH, D_MODEL), const),
            pl.BlockSpec((PROJ_ROWS, LANES), lambda i: (i % pos_blocks, 0)),
            pl.BlockSpec((PROJ_ROWS, LANES), lambda i: (i % pos_blocks, 0)),
            pl.BlockSpec((DIFF_QK_DIM, PROJ_ROWS), lambda i: (0, i % pos_blocks)),
            pl.BlockSpec((DIFF_QK_DIM, PROJ_ROWS), lambda i: (0, i % pos_blocks)),
        ],
        out_specs=[
            pl.BlockSpec((PROJ_ROWS, ROW_WIDTH), lambda i: (i, 0)),
            pl.BlockSpec((1, T_WIDTH, PROJ_ROWS), lambda i: (i, 0, 0)),
            pl.BlockSpec((sb_per_step, TS_WIDTH, SB_BLOCK), lambda i: (i, 0, 0)),
        ],
        out_shape=[
            jax.ShapeDtypeStruct((n, ROW_WIDTH), BF16),
            jax.ShapeDtypeStruct((n // PROJ_ROWS, T_WIDTH, PROJ_ROWS), BF16),
            jax.ShapeDtypeStruct((n // SB_BLOCK, TS_WIDTH, SB_BLOCK), BF16),
        ],
        compiler_params=pltpu.CompilerParams(
            dimension_semantics=("parallel",), vmem_limit_bytes=VMEM_LIMIT),
        name="in_projection",
    )(x2, g, w_rows, w_feat, w_feat_sb, cos_t, sin_t, cos_f, sin_f)


def _sb_kernel(qt_ref, k_ref, vt_ref, o_ref):
    i = pl.program_id(1)
    n_pairs = SB_WIDTH // LANES
    feat = lax.broadcasted_iota(jnp.int32, (LANES, SB_BLOCK), 0)
    pos = lax.broadcasted_iota(jnp.int32, (SB_BLOCK, SB_BLOCK), 0)
    qry = lax.broadcasted_iota(jnp.int32, (SB_BLOCK, SB_BLOCK), 1)
    key = (pos % 8) * SB_RUN + pos // 8
    strict = key < qry
    first_head = feat < HEAD_DIM

    def head_queries(p):
        qt = qt_ref[0, p * LANES:(p + 1) * LANES, :]
        zero = jnp.zeros_like(qt)
        return jnp.where(first_head, qt, zero), jnp.where(first_head, zero, qt)

    qts = [q for p in range(n_pairs) for q in head_queries(p)]

    def pair_lanes(head):
        return slice((head // 2) * LANES, (head // 2 + 1) * LANES)

    def scores(j, head):
        start = pl.multiple_of(j * SB_BLOCK, SB_BLOCK)
        return jnp.dot(k_ref[pl.ds(start, SB_BLOCK), pair_lanes(head)], qts[head],
                       preferred_element_type=F32)

    def block(j, head, z, run, masked):
        e = jnp.exp2(jnp.minimum(z, SB_MAX_LOG2))
        keep = 1.0 / (1.0 + e)
        beta = e * keep
        if masked:
            keep = jnp.where(strict, keep, 1.0)
            beta = jnp.where(strict, beta, 0.0)
        keep3 = keep.reshape(SB_RUN, 8, SB_BLOCK)
        beta3 = beta.reshape(SB_RUN, 8, SB_BLOCK)
        after = [None] * SB_RUN
        acc = jnp.ones((8, SB_BLOCK), F32)
        for v in range(SB_RUN - 1, -1, -1):
            after[v] = acc
            acc = acc * keep3[v]
        later = run
        offs = [None] * 8
        for r in range(7, -1, -1):
            offs[r] = later
            later = later * acc[r:r + 1]
        base = jnp.concatenate(offs, axis=0)
        a = jnp.concatenate([beta3[v] * after[v] * base for v in range(SB_RUN)], axis=0)
        pv = jnp.dot(vt_ref[j, pair_lanes(head), :], a.astype(BF16),
                     preferred_element_type=F32)
        return pv, later

    def pair_rows(pv_first, pv_second):
        return jnp.where(first_head, pv_first, pv_second)

    has_prev = i >= 1
    prev = jnp.maximum(i - 1, 0)
    z_diag = [scores(i, head) for head in range(SB_HEADS)]
    z_prev = [scores(prev, head) for head in range(SB_HEADS)]
    runs, pvs = [], []
    for head in range(SB_HEADS):
        pv_d, run_d = block(i, head, z_diag[head], jnp.ones((1, SB_BLOCK), F32), True)
        pv_p, run_p = block(prev, head, z_prev[head], run_d, False)
        pvs.append(pv_d + jnp.where(has_prev, pv_p, 0.0))
        runs.append(run_p)
    accs = [pair_rows(pvs[2 * p], pvs[2 * p + 1]) for p in range(n_pairs)]

    def alive(rs):
        top = rs[0]
        for r in rs[1:]:
            top = jnp.maximum(top, r)
        return (jnp.max(top) >= SB_MIN_WEIGHT).astype(jnp.int32)

    def cond(c):
        return (c[0] >= 0) & (c[1] > 0)

    def body(c):
        j, old_runs, old_accs = c[0], c[2:2 + SB_HEADS], c[2 + SB_HEADS:]
        new = [block(j, head, scores(j, head), old_runs[head], False)
               for head in range(SB_HEADS)]
        new_runs = [run for _, run in new]
        new_accs = [old_accs[p] + pair_rows(new[2 * p][0], new[2 * p + 1][0])
                    for p in range(n_pairs)]
        return (j - 1, alive(new_runs), *new_runs, *new_accs)

    out = lax.while_loop(cond, body, (i - 2, alive(runs), *runs, *accs))
    for p in range(n_pairs):
        o_ref[:, p * LANES:(p + 1) * LANES] = out[2 + SB_HEADS + p].T.astype(o_ref.dtype)


def _sb_attention(proj, proj_ts, batch, seq):
    nq = seq // SB_BLOCK
    q_rb, k_cb, v_rb = (SBQ_RB * LANES // SB_WIDTH, SBK_CB * LANES // SB_WIDTH,
                        SBV_RB * LANES // SB_WIDTH)
    return pl.pallas_call(
        _sb_kernel,
        grid=(batch, nq),
        in_specs=[
            pl.BlockSpec((1, SB_WIDTH, SB_BLOCK), lambda b, i: (b * nq + i, q_rb, 0)),
            pl.BlockSpec((seq, SB_WIDTH), lambda b, i: (b, k_cb)),
            pl.BlockSpec((nq, SB_WIDTH, SB_BLOCK), lambda b, i: (b, v_rb, 0)),
        ],
        out_specs=pl.BlockSpec((SB_BLOCK, SB_WIDTH), lambda b, i: (b * nq + i, 0)),
        out_shape=jax.ShapeDtypeStruct((batch * seq, SB_WIDTH), BF16),
        compiler_params=pltpu.CompilerParams(
            dimension_semantics=("parallel", "arbitrary"),
            vmem_limit_bytes=VMEM_LIMIT),
        name="stickbreak_attention",
    )(proj_ts, proj, proj_ts)


def _diff_kernel(lam_ref, gs_ref, qt_ref, k_ref, vt_ref, o_ref,
                 acc_ref, m_ref, l_ref, sa_ref, sb_ref, *, lambda_init):
    i = pl.program_id(1)
    feat = lax.broadcasted_iota(jnp.int32, (LANES, DIFF_BLOCK), 0)
    key = lax.broadcasted_iota(jnp.int32, (DIFF_BLOCK, DIFF_BLOCK), 0)
    qry = lax.broadcasted_iota(jnp.int32, (DIFF_BLOCK, DIFF_BLOCK), 1)
    causal = key <= qry
    first_comp = feat < DIFF_QK_DIM

    def head_queries(h):
        qt = qt_ref[0, h * LANES:(h + 1) * LANES, :]
        zero = jnp.zeros_like(qt)
        return jnp.where(first_comp, qt, zero), jnp.where(first_comp, zero, qt)

    qts = [q for h in range(DIFF_HEADS) for q in head_queries(h)]

    lv = lam_ref[...]
    lam = (jnp.exp(jnp.sum(lv[0:1] * lv[1:2], axis=-1, keepdims=True))
           - jnp.exp(jnp.sum(lv[2:3] * lv[3:4], axis=-1, keepdims=True)) + lambda_init)

    def head_lanes(h):
        return slice(h * LANES, (h + 1) * LANES)

    def scores(j, s_ref):
        start = pl.multiple_of(j * DIFF_BLOCK, DIFF_BLOCK)
        for h in range(DIFF_HEADS):
            kb = k_ref[pl.ds(start, DIFF_BLOCK), head_lanes(h)]
            for c in range(2):
                s_ref[2 * h + c] = jnp.dot(kb, qts[2 * h + c], preferred_element_type=F32)

    def absorb(j, s_ref, masked):
        for h in range(DIFF_HEADS):
            vt = vt_ref[j, head_lanes(h), :]
            for c in range(2):
                n = 2 * h + c
                m = m_ref[n]
                s = s_ref[n]
                if masked:
                    s = jnp.where(causal, s, -jnp.inf)
                m_new = jnp.maximum(m, jnp.max(s, axis=0, keepdims=True))
                alpha = jnp.exp2(m - m_new)
                p = jnp.exp2(s - m_new)
                m_ref[n] = m_new
                l_ref[n] = alpha * l_ref[n] + jnp.sum(p, axis=0, keepdims=True)
                acc_ref[n] = alpha * acc_ref[n] + jnp.dot(vt, p.astype(BF16),
                                                          preferred_element_type=F32)

    acc_ref[...] = jnp.zeros_like(acc_ref)
    l_ref[...] = jnp.zeros_like(l_ref)
    m_ref[...] = jnp.full(m_ref.shape, -jnp.inf, F32)

    scores(0, sa_ref)

    def pair(t, _):
        b = 2 * t
        scores(b + 1, sb_ref)
        absorb(b, sa_ref, False)
        scores(b + 2, sa_ref)
        absorb(b + 1, sb_ref, False)
        return 0

    lax.fori_loop(0, i // 2, pair, 0)

    @pl.when(i % 2 == 0)
    def _():
        absorb(i, sa_ref, True)

    @pl.when(i % 2 == 1)
    def _():
        scores(i, sb_ref)
        absorb(i - 1, sa_ref, False)
        absorb(i, sb_ref, True)

    for h in range(DIFF_HEADS):
        ot = (acc_ref[2 * h] / l_ref[2 * h]
              - lam * (acc_ref[2 * h + 1] / l_ref[2 * h + 1]))
        ms = jnp.mean(ot * ot, axis=0, keepdims=True)
        ot = ot * lax.rsqrt(ms + EPS) * gs_ref[...] * (1.0 - lambda_init)
        o_ref[:, head_lanes(h)] = ot.T.astype(o_ref.dtype)


def _diff_attention(proj, proj_t, lam_rows, g_subln_col, batch, seq, lambda_init):
    nq = seq // DIFF_BLOCK
    n_chains = 2 * DIFF_HEADS
    q_rb, k_cb, v_rb = (DQ_RB * LANES // DIFF_V_WIDTH, DK_CB * LANES // DIFF_V_WIDTH,
                        DV_RB * LANES // DIFF_V_WIDTH)
    return pl.pallas_call(
        functools.partial(_diff_kernel, lambda_init=lambda_init),
        grid=(batch, nq),
        in_specs=[
            pl.BlockSpec((4, DIFF_QK_DIM), lambda b, i: (0, 0)),
            pl.BlockSpec((DIFF_V_DIM, 1), lambda b, i: (0, 0)),
            pl.BlockSpec((1, DIFF_QK_WIDTH, DIFF_BLOCK), lambda b, i: (b * nq + i, q_rb, 0)),
            pl.BlockSpec((seq, DIFF_QK_WIDTH), lambda b, i: (b, k_cb)),
            pl.BlockSpec((nq, DIFF_V_WIDTH, DIFF_BLOCK), lambda b, i: (b, v_rb, 0)),
        ],
        out_specs=pl.BlockSpec((DIFF_BLOCK, DIFF_V_WIDTH), lambda b, i: (b * nq + i, 0)),
        out_shape=jax.ShapeDtypeStruct((batch * seq, DIFF_V_WIDTH), BF16),
        scratch_shapes=[pltpu.VMEM((n_chains, DIFF_V_DIM, DIFF_BLOCK), F32),
                        pltpu.VMEM((n_chains, 1, DIFF_BLOCK), F32),
                        pltpu.VMEM((n_chains, 1, DIFF_BLOCK), F32),
                        pltpu.VMEM((n_chains, DIFF_BLOCK, DIFF_BLOCK), F32),
                        pltpu.VMEM((n_chains, DIFF_BLOCK, DIFF_BLOCK), F32)],
        compiler_params=pltpu.CompilerParams(
            dimension_semantics=("parallel", "arbitrary"),
            vmem_limit_bytes=VMEM_LIMIT),
        name="differential_attention",
    )(lam_rows, g_subln_col, proj_t, proj, proj_t)


def _post_kernel(oa_ref, ob_ref, ga_ref, gb_ref, x_ref, wua_ref, wub_ref, wo_ref, gn_ref,
                 wr_hi_ref, wr_lo_ref, br_ref,
                 x1_ref, h2_ref, slab_ref, cnt_ref, carry_ref):
    step_id = pl.program_id(0)

    @pl.when(step_id == 0)
    def _():
        carry_ref[...] = jnp.zeros_like(carry_ref)

    u_a = jnp.dot(oa_ref[...], wua_ref[...], preferred_element_type=F32)
    u_b = jnp.dot(ob_ref[...], wub_ref[...], preferred_element_type=F32)
    y = (jax.nn.sigmoid(ga_ref[...].astype(F32)) * u_a
         + jax.nn.sigmoid(gb_ref[...].astype(F32)) * u_b)
    x1 = x_ref[...] + jnp.dot(y.astype(BF16), wo_ref[...], preferred_element_type=F32)
    x1_ref[...] = x1
    ms = jnp.mean(x1 * x1, axis=-1, keepdims=True)
    h2 = x1 * lax.rsqrt(ms + EPS) * gn_ref[...]
    _store_row_tiles(h2_ref, h2)

    h_hi = h2.astype(BF16)
    h_lo = (h2 - h_hi.astype(F32)).astype(BF16)
    logits = (jnp.dot(h_hi, wr_hi_ref[...], preferred_element_type=F32)
              + jnp.dot(h_lo, wr_hi_ref[...], preferred_element_type=F32)
              + jnp.dot(h_hi, wr_lo_ref[...], preferred_element_type=F32)
              + br_ref[...])

    lane = lax.broadcasted_iota(jnp.int32, (POST_ROWS, LANES), 1)
    neg = -jnp.inf

    def first_argmax(v):
        m = jnp.max(v, axis=-1, keepdims=True)
        idx = jnp.min(jnp.where(v == m, lane, LANES), axis=-1, keepdims=True)
        return m, idx

    g_logits = jnp.where(lane < N_GROUPS, logits, neg)
    g_max, grp = first_argmax(g_logits)
    p_grp = 1.0 / jnp.sum(jnp.exp(g_logits - g_max), axis=-1, keepdims=True)

    lo_lane = ROUTER_EXPERT_LANE0 + grp * EXPERTS_PER_GROUP
    in_grp = (lane >= lo_lane) & (lane < lo_lane + EXPERTS_PER_GROUP)
    e_logits = jnp.where(in_grp, logits, neg)
    m1, i1 = first_argmax(e_logits)
    m2, i2 = first_argmax(jnp.where(lane == i1, neg, e_logits))
    r = jnp.exp(m2 - m1)
    w0 = p_grp / (1.0 + r)
    w1 = p_grp * r / (1.0 + r)

    hit0 = lane == i1
    hit1 = lane == i2
    onehot = jnp.where(hit0 | hit1, 1.0, 0.0)
    trow = lax.broadcasted_iota(jnp.int32, (POST_ROWS, POST_ROWS), 0)
    tcol = lax.broadcasted_iota(jnp.int32, (POST_ROWS, POST_ROWS), 1)
    earlier = jnp.where(tcol < trow, 1.0, 0.0).astype(BF16)
    before = carry_ref[0:1, :] + jnp.dot(earlier, onehot.astype(BF16),
                                         preferred_element_type=F32)
    r0 = jnp.sum(jnp.where(hit0, before, 0.0), axis=-1, keepdims=True)
    r1 = jnp.sum(jnp.where(hit1, before, 0.0), axis=-1, keepdims=True)
    new_carry = carry_ref[0:1, :] + jnp.sum(onehot, axis=0, keepdims=True)
    carry_ref[0:1, :] = new_carry
    cnt_ref[...] = jnp.broadcast_to(new_carry, cnt_ref.shape)

    e0 = (i1 - ROUTER_EXPERT_LANE0).astype(F32)
    e1 = (i2 - ROUTER_EXPERT_LANE0).astype(F32)
    slab = jnp.zeros((POST_ROWS, LANES), F32)
    for pos, val in ((SLAB_E0, e0), (SLAB_E1, e1), (SLAB_R0, r0), (SLAB_R1, r1),
                     (SLAB_W0, w0), (SLAB_W1, w1)):
        slab = jnp.where(lane == pos, val, slab)
    slab_ref[...] = slab


def _post_attention(o_a, o_b, proj, x2, wua, wub, wo, gn, wr_hi, wr_lo, br):
    n = x2.shape[0]
    const = lambda i: (0, 0)
    return pl.pallas_call(
        _post_kernel,
        grid=(n // POST_ROWS,),
        in_specs=[
            pl.BlockSpec((POST_ROWS, SB_WIDTH), lambda i: (i, 0)),
            pl.BlockSpec((POST_ROWS, DIFF_V_WIDTH), lambda i: (i, 0)),
            pl.BlockSpec((POST_ROWS, D_MODEL), lambda i: (i, GATE_A_B1024)),
            pl.BlockSpec((POST_ROWS, D_MODEL), lambda i: (i, GATE_B_B1024)),
            pl.BlockSpec((POST_ROWS, D_MODEL), lambda i: (i, 0)),
            pl.BlockSpec((SB_WIDTH, D_MODEL), const),
            pl.BlockSpec((DIFF_V_WIDTH, D_MODEL), const),
            pl.BlockSpec((D_MODEL, D_MODEL), const),
            pl.BlockSpec((1, D_MODEL), const),
            pl.BlockSpec((D_MODEL, LANES), const),
            pl.BlockSpec((D_MODEL, LANES), const),
            pl.BlockSpec((1, LANES), const),
        ],
        out_specs=[
            pl.BlockSpec((POST_ROWS, D_MODEL), lambda i: (i, 0)),
            pl.BlockSpec((POST_ROWS * ROW_TILE, LANES), lambda i: (i, 0)),
            pl.BlockSpec((POST_ROWS, LANES), lambda i: (i, 0)),
            pl.BlockSpec((8, LANES), const),
        ],
        out_shape=[
            jax.ShapeDtypeStruct((n, D_MODEL), F32),
            jax.ShapeDtypeStruct((n * ROW_TILE, LANES), F32),
            jax.ShapeDtypeStruct((n, LANES), F32),
            jax.ShapeDtypeStruct((8, LANES), F32),
        ],
        scratch_shapes=[pltpu.VMEM((8, LANES), F32)],
        compiler_params=pltpu.CompilerParams(
            dimension_semantics=("arbitrary",), vmem_limit_bytes=VMEM_LIMIT),
        name="merge_route",
    )(o_a, o_b, proj, proj, x2, wua, wub, wo, gn, wr_hi, wr_lo, br)


def _issue_row_copies(make_copy):
    def group(g, _):
        for u in range(ROW_DMA_UNROLL):
            for k in range(TOP_K):
                make_copy(g * ROW_DMA_UNROLL + u, k).start(priority=k % 2)
        return 0

    lax.fori_loop(0, ROW_DMA_TOKENS // ROW_DMA_UNROLL, group, 0)


def _scatter_kernel(dest_ref, pad_ref, h_ref, xd_ref, zero_tile, sem):
    n_pad = pad_ref.shape[-1]
    zero_tile[...] = jnp.zeros_like(zero_tile)
    _issue_row_copies(
        lambda t, k: _row_tile_copy(h_ref, t, xd_ref, dest_ref[0, 0, t * TOP_K + k], sem))
    for z in range(n_pad):
        _row_tile_copy(zero_tile, 0, xd_ref, pad_ref[0, 0, z], sem).start(priority=z % 2)
    for _ in range(TOP_K):
        pltpu.make_async_copy(
            h_ref, xd_ref.at[pl.ds(0, ROW_DMA_TOKENS * ROW_TILE), :], sem).wait()
    pltpu.make_async_copy(
        h_ref.at[pl.ds(0, n_pad * ROW_TILE), :], xd_ref.at[pl.ds(0, n_pad * ROW_TILE), :],
        sem).wait()


def _scatter_rows(dest3, pad3, h2_tiles, n_rows):
    n = h2_tiles.shape[0] // ROW_TILE
    steps = n // ROW_DMA_TOKENS
    assert pad3.shape[0] == steps and pad3.shape[2] <= ROW_DMA_TOKENS
    return pl.pallas_call(
        _scatter_kernel,
        grid=(steps,),
        in_specs=[
            pl.BlockSpec((1, 1, TOP_K * ROW_DMA_TOKENS), lambda i: (i, 0, 0),
                         memory_space=pltpu.SMEM),
            pl.BlockSpec((1, 1, pad3.shape[2]), lambda i: (i, 0, 0), memory_space=pltpu.SMEM),
            pl.BlockSpec((ROW_DMA_TOKENS * ROW_TILE, LANES), lambda i: (i, 0)),
        ],
        out_specs=pl.BlockSpec(memory_space=pl.ANY),
        out_shape=jax.ShapeDtypeStruct((n_rows * ROW_TILE, LANES), F32),
        scratch_shapes=[pltpu.VMEM((ROW_TILE, LANES), F32), pltpu.SemaphoreType.DMA],
        compiler_params=pltpu.CompilerParams(
            dimension_semantics=("arbitrary",), vmem_limit_bytes=VMEM_LIMIT),
        name="scatter_rows",
    )(dest3, pad3, h2_tiles)


def _expert_kernel(be_ref, nb_ref, x_ref, *refs):
    w_refs, y_ref = refs[:-1], refs[-1]
    first_block = pl.program_id(0) * MOE_GROUP

    @pl.when(first_block < nb_ref[0])
    def _():
        for blk in range(MOE_GROUP):
            wg_ref, wu_ref, wd_ref = w_refs[3 * blk:3 * blk + 3]
            x = _load_row_tiles(x_ref, MOE_BLOCK, first_row=blk * MOE_BLOCK).astype(BF16)
            g = jnp.dot(x, wg_ref[0], preferred_element_type=F32)
            u = jnp.dot(x, wu_ref[0], preferred_element_type=F32)
            hmid = (g * jax.nn.sigmoid(g) * u).astype(BF16)
            _store_row_tiles(y_ref, jnp.dot(hmid, wd_ref[0], preferred_element_type=F32),
                             first_row=blk * MOE_BLOCK)

    @pl.when(first_block >= nb_ref[0])
    def _():
        y_ref[...] = jnp.zeros_like(y_ref)


def _expert_mlp(block_e, n_used, xd_tiles, wg, wu, wd):
    n_blocks = xd_tiles.shape[0] // (MOE_BLOCK * ROW_TILE)
    assert n_blocks % MOE_GROUP == 0
    step_rows = MOE_GROUP * MOE_BLOCK * ROW_TILE

    def w_spec(shape, blk):
        return pl.BlockSpec((1,) + shape, lambda i, be, nb: (be[i * MOE_GROUP + blk], 0, 0))

    w_specs, w_args = [], []
    for blk in range(MOE_GROUP):
        w_specs += [w_spec((D_MODEL, D_EXPERT), blk), w_spec((D_MODEL, D_EXPERT), blk),
                    w_spec((D_EXPERT, D_MODEL), blk)]
        w_args += [wg, wu, wd]
    return pl.pallas_call(
        _expert_kernel,
        grid_spec=pltpu.PrefetchScalarGridSpec(
            num_scalar_prefetch=2,
            grid=(n_blocks // MOE_GROUP,),
            in_specs=[pl.BlockSpec((step_rows, LANES), lambda i, be, nb: (i, 0))] + w_specs,
            out_specs=pl.BlockSpec((step_rows, LANES), lambda i, be, nb: (i, 0)),
        ),
        out_shape=jax.ShapeDtypeStruct(xd_tiles.shape, F32),
        compiler_params=pltpu.CompilerParams(
            dimension_semantics=("arbitrary",), vmem_limit_bytes=VMEM_LIMIT),
        name="expert_mlp",
    )(block_e, n_used, xd_tiles, *w_args)


def _combine_kernel(dest_ref, next_dest_ref, x1_ref, slab_ref, g_ref, yd_ref, o_ref, ybuf, sems):
    step = pl.program_id(0)
    slot_rows = TOP_K * ROW_DMA_TOKENS

    def gather(idx_ref, slot):
        _issue_row_copies(
            lambda t, k: _row_tile_copy(yd_ref, idx_ref[0, 0, t * TOP_K + k], ybuf,
                                        slot * slot_rows + k * ROW_DMA_TOKENS + t,
                                        sems.at[slot]))

    @pl.when(step == 0)
    def _():
        gather(dest_ref, 0)

    @pl.when(step + 1 < pl.num_programs(0))
    def _():
        gather(next_dest_ref, (step + 1) % 2)

    slot = step % 2
    first = slot * slot_rows
    pltpu.make_async_copy(
        yd_ref.at[pl.ds(0, slot_rows * ROW_TILE), :],
        ybuf.at[pl.ds(pl.multiple_of(first * ROW_TILE, ROW_TILE), slot_rows * ROW_TILE), :],
        sems.at[slot]).wait()

    slab = slab_ref[...]
    w0 = slab[:, SLAB_W0:SLAB_W0 + 1]
    w1 = slab[:, SLAB_W1:SLAB_W1 + 1]
    x = (x1_ref[...] + w0 * _load_row_tiles(ybuf, ROW_DMA_TOKENS, first_row=first)
         + w1 * _load_row_tiles(ybuf, ROW_DMA_TOKENS, first_row=first + ROW_DMA_TOKENS))
    ms = jnp.mean(x * x, axis=-1, keepdims=True)
    o_ref[...] = x * lax.rsqrt(ms + EPS) * g_ref[...]


def _combine(dest3, x1, slab, g_final, yd_tiles):
    n = x1.shape[0]
    steps = n // ROW_DMA_TOKENS
    return pl.pallas_call(
        _combine_kernel,
        grid=(steps,),
        in_specs=[
            pl.BlockSpec((1, 1, TOP_K * ROW_DMA_TOKENS), lambda i: (i, 0, 0),
                         memory_space=pltpu.SMEM),
            pl.BlockSpec((1, 1, TOP_K * ROW_DMA_TOKENS),
                         lambda i: (jnp.minimum(i + 1, steps - 1), 0, 0),
                         memory_space=pltpu.SMEM),
            pl.BlockSpec((ROW_DMA_TOKENS, D_MODEL), lambda i: (i, 0)),
            pl.BlockSpec((ROW_DMA_TOKENS, LANES), lambda i: (i, 0)),
            pl.BlockSpec((1, D_MODEL), lambda i: (0, 0)),
            pl.BlockSpec(memory_space=pl.ANY),
        ],
        out_specs=pl.BlockSpec((ROW_DMA_TOKENS, D_MODEL), lambda i: (i, 0)),
        out_shape=jax.ShapeDtypeStruct((n, D_MODEL), F32),
        scratch_shapes=[pltpu.VMEM((2 * TOP_K * ROW_DMA_TOKENS * ROW_TILE, LANES), F32),
                        pltpu.SemaphoreType.DMA((2,))],
        compiler_params=pltpu.CompilerParams(
            dimension_semantics=("arbitrary",), vmem_limit_bytes=VMEM_LIMIT),
        name="combine_norm",
    )(dest3, dest3, x1, slab, g_final, yd_tiles)


def _rope_tables(seq):
    pos = jnp.arange(seq, dtype=F32)
    inv_freq = 1.0 / (ROPE_THETA ** (jnp.arange(0, DIFF_QK_DIM, 2, dtype=F32) / DIFF_QK_DIM))
    freqs = pos[:, None] * inv_freq[None, :]
    emb = jnp.concatenate([freqs, freqs], axis=-1)
    cos, sin = jnp.cos(emb), jnp.sin(emb)
    half = DIFF_QK_DIM // 2
    sin_signed = jnp.concatenate([-sin[:, :half], sin[:, half:]], axis=-1)
    reps = LANES // DIFF_QK_DIM
    return (jnp.tile(cos, (1, reps)), jnp.tile(sin_signed, (1, reps)), cos.T, sin_signed.T)


def _split_in_projection(w):
    edges = [0]
    for width in (SB_WIDTH, SB_WIDTH, SB_WIDTH, DIFF_QK_WIDTH, DIFF_QK_WIDTH, DIFF_V_WIDTH,
                  D_MODEL, D_MODEL):
        edges.append(edges[-1] + width)
    sb_q, sb_k, sb_v, d_q, d_k, d_v, gate_a, gate_b = (
        w[:, lo:hi] for lo, hi in zip(edges[:-1], edges[1:]))
    w_rows = jnp.concatenate([sb_k, d_k, gate_a, gate_b], axis=1)
    w_feat = jnp.concatenate([d_q, d_v], axis=1).T
    w_feat_sb = jnp.concatenate([sb_q, sb_v], axis=1).T
    return w_rows.astype(BF16), w_feat.astype(BF16), w_feat_sb.astype(BF16)


def _split_bf16(w):
    hi = w.astype(BF16)
    lo = (w - hi.astype(F32)).astype(BF16)
    return hi, lo


def kernel(x, g_norm_mix, w_in, lambda_q1, lambda_k1, lambda_q2, lambda_k2, g_subln,
           w_up_a, w_up_b, w_out, g_norm_ffn, w_router_group, b_router_group,
           w_router_expert, b_router_expert, w_expert_gate, w_expert_up, w_expert_down,
           g_norm_final):
    batch, seq, d = x.shape
    depth = w_in.shape[0]
    n_tok = batch * seq
    assert depth == 1, "the final RMSNorm is fused into the layer's combine step"
    assert d == D_MODEL and seq % DIFF_BLOCK == 0 and seq % PROJ_ROWS == 0
    assert seq % SB_BLOCK == 0 and PROJ_ROWS % SB_BLOCK == 0
    assert DIFF_BLOCK == PROJ_ROWS, "feature-major q/v blocks are written per projection step"
    assert (N_EXPERTS * MOE_BLOCK) % (n_tok // ROW_DMA_TOKENS) == 0
    assert n_tok % POST_ROWS == 0 and n_tok % ROW_DMA_TOKENS == 0
    n_assign = n_tok * TOP_K
    n_blocks = n_assign // MOE_BLOCK + N_EXPERTS
    n_rows = n_blocks * MOE_BLOCK

    rope = _rope_tables(seq)
    x2 = x.reshape(n_tok, d)
    for l in range(depth):
        lambda_init = 0.8 - 0.6 * math.exp(-0.3 * l)
        w_rows, w_feat, w_feat_sb = _split_in_projection(w_in[l])
        proj, proj_t, proj_ts = _in_projection(x2, g_norm_mix[l][None, :], w_rows, w_feat,
                                               w_feat_sb, rope, seq)
        o_a = _sb_attention(proj, proj_ts, batch, seq)
        lam_rows = jnp.stack([lambda_q1[l], lambda_k1[l], lambda_q2[l], lambda_k2[l]])
        o_b = _diff_attention(proj, proj_t, lam_rows, g_subln[l][:, None], batch, seq,
                              lambda_init)

        w_r = jnp.zeros((d, LANES), F32)
        w_r = w_r.at[:, :N_GROUPS].set(w_router_group[l])
        w_r = w_r.at[:, N_GROUPS:N_GROUPS + N_EXPERTS].set(w_router_expert[l])
        b_r = jnp.zeros((1, LANES), F32)
        b_r = b_r.at[0, :N_GROUPS].set(b_router_group[l])
        b_r = b_r.at[0, N_GROUPS:N_GROUPS + N_EXPERTS].set(b_router_expert[l])
        wr_hi, wr_lo = _split_bf16(w_r)

        x1, h2, slab, cnt = _post_attention(
            o_a, o_b, proj, x2, w_up_a[l].astype(BF16), w_up_b[l].astype(BF16),
            w_out[l].astype(BF16), g_norm_ffn[l][None, :], wr_hi, wr_lo, b_r)

        counts = cnt[0, N_GROUPS:N_GROUPS + N_EXPERTS].astype(jnp.int32)
        padded = (counts + MOE_BLOCK - 1) // MOE_BLOCK * MOE_BLOCK
        pad_end = jnp.cumsum(padded)
        pad_start = pad_end - padded
        e_ids = slab[:, SLAB_E0:SLAB_E1 + 1].astype(jnp.int32)
        ranks = slab[:, SLAB_R0:SLAB_R1 + 1].astype(jnp.int32)
        expert_iota = jnp.arange(N_EXPERTS, dtype=jnp.int32)
        dest = ranks + jnp.sum(
            jnp.where(e_ids[:, :, None] == expert_iota, pad_start, 0), axis=-1)
        steps = n_tok // ROW_DMA_TOKENS
        dest3 = dest.reshape(steps, 1, TOP_K * ROW_DMA_TOKENS)
        gap = padded - counts
        gap_end = jnp.cumsum(gap)
        z = jnp.arange(n_rows - n_assign, dtype=jnp.int32)
        seg = jnp.sum((gap_end[None, :] <= z[:, None]).astype(jnp.int32), axis=1)
        seg_onehot = seg[:, None] == jnp.arange(N_EXPERTS + 1, dtype=jnp.int32)
        seg_first_row = jnp.concatenate([pad_start + counts, pad_end[-1:]])
        seg_first_z = jnp.concatenate([gap_end - gap, gap_end[-1:]])
        pad_rows = z + jnp.sum(jnp.where(seg_onehot, seg_first_row - seg_first_z, 0), axis=1)
        pad3 = pad_rows.reshape(steps, 1, (n_rows - n_assign) // steps)
        block_start = jnp.arange(n_blocks, dtype=jnp.int32) * MOE_BLOCK
        block_e = jnp.minimum(
            jnp.sum((pad_end[None, :] <= block_start[:, None]).astype(jnp.int32), axis=1),
            N_EXPERTS - 1)
        n_used = (pad_end[-1:] // MOE_BLOCK).astype(jnp.int32)

        xd = _scatter_rows(dest3, pad3, h2, n_rows)
        yd = _expert_mlp(block_e, n_used, xd, w_expert_gate[l].astype(BF16),
                         w_expert_up[l].astype(BF16), w_expert_down[l].astype(BF16))
        x2 = _combine(dest3, x1, slab, g_norm_final[None, :], yd)
    return x2.reshape(batch, seq, d)
```

```python
import functools
import math

import jax
import jax.numpy as jnp
from jax import lax
from jax.experimental import pallas as pl
from jax.experimental.pallas import tpu as pltpu

D_MODEL = 1024
HEAD_DIM = 64
SB_HEADS = 8
SB_WIDTH = SB_HEADS * HEAD_DIM
DIFF_HEADS = 4
DIFF_QK_DIM = 64
DIFF_V_DIM = 2 * DIFF_QK_DIM
DIFF_QK_WIDTH = DIFF_HEADS * 2 * DIFF_QK_DIM
DIFF_V_WIDTH = DIFF_HEADS * DIFF_V_DIM
IN_WIDTH = 3 * SB_WIDTH + 2 * DIFF_QK_WIDTH + DIFF_V_WIDTH + 2 * D_MODEL
ROPE_THETA = 10000.0
N_GROUPS = 4
EXPERTS_PER_GROUP = 8
N_EXPERTS = N_GROUPS * EXPERTS_PER_GROUP
TOP_K = 2
D_EXPERT = 256
EPS = 1e-6

LANES = 128
QK_SCALE = HEAD_DIM ** -0.5
LOG2E = math.log2(math.e)

F32 = jnp.float32
BF16 = jnp.bfloat16

PROJ_ROWS = 512
PROJ_CHUNK = 512
SB_BLOCK = 256
DIFF_BLOCK = 512
SB_MIN_WEIGHT = 2.0 ** -126
SB_MAX_LOG2 = 126.0
POST_ROWS = 512
MOE_BLOCK = 256
MOE_GROUP = 4
ROW_DMA_TOKENS = 512
ROW_DMA_UNROLL = 8
VMEM_LIMIT = 56 * 1024 * 1024

SB_RUN = SB_BLOCK // 8
ROW_WIDTH = SB_WIDTH + DIFF_QK_WIDTH + 2 * D_MODEL
T_WIDTH = DIFF_QK_WIDTH + DIFF_V_WIDTH
TS_WIDTH = 2 * SB_WIDTH
SBK_CB, DK_CB = 0, 4
GATE_A_B1024, GATE_B_B1024 = 1, 2
DQ_RB, DV_RB = 0, DIFF_QK_WIDTH // LANES
SBQ_RB, SBV_RB = 0, SB_WIDTH // LANES

SLAB_E0, SLAB_E1, SLAB_R0, SLAB_R1, SLAB_W0, SLAB_W1 = 0, 1, 2, 3, 4, 5
ROUTER_EXPERT_LANE0 = N_GROUPS


def _nt_dot(a, b):
    return lax.dot_general(a, b, (((1,), (1,)), ((), ())), preferred_element_type=F32)


ROW_TILE = D_MODEL // LANES


def _store_row_tiles(ref, value, first_row=0):
    rows = value.shape[0]
    for k in range(ROW_TILE):
        ref[pl.ds(first_row * ROW_TILE + k, rows, stride=ROW_TILE), :] = (
            value[:, k * LANES:(k + 1) * LANES])


def _load_row_tiles(ref, rows, first_row=0):
    return jnp.concatenate(
        [ref[pl.ds(first_row * ROW_TILE + k, rows, stride=ROW_TILE), :]
         for k in range(ROW_TILE)], axis=1)


def _row_tile_copy(src_ref, src_row, dst_ref, dst_row, sem):
    src = src_ref.at[pl.ds(pl.multiple_of(src_row * ROW_TILE, ROW_TILE), ROW_TILE), :]
    dst = dst_ref.at[pl.ds(pl.multiple_of(dst_row * ROW_TILE, ROW_TILE), ROW_TILE), :]
    return pltpu.make_async_copy(src, dst, sem)


def _inproj_kernel(x_ref, g_ref, w_ref, wt_ref, wts_ref, cos_ref, sin_ref, cost_ref, sint_ref,
                   o_ref, ot_ref, ots_ref):
    def normed(x):
        ms = jnp.mean(x * x, axis=-1, keepdims=True)
        return (x * lax.rsqrt(ms + EPS) * g_ref[...]).astype(BF16)

    h = normed(x_ref[...])
    dst = lax.broadcasted_iota(jnp.int32, (SB_BLOCK, SB_BLOCK), 0)
    src = lax.broadcasted_iota(jnp.int32, (SB_BLOCK, SB_BLOCK), 1)
    perm = jnp.where(src == (dst % 8) * SB_RUN + dst // 8, 1.0, 0.0).astype(BF16)
    h_perm = jnp.concatenate(
        [jnp.dot(perm, h[base:base + SB_BLOCK], preferred_element_type=F32).astype(BF16)
         for base in range(0, PROJ_ROWS, SB_BLOCK)], axis=0)

    lane = lax.broadcasted_iota(jnp.int32, (PROJ_ROWS, LANES), 1)
    first_half = (lane % DIFF_QK_DIM) < (DIFF_QK_DIM // 2)
    cos = cos_ref[...]
    sin = sin_ref[...]

    def rope(acc):
        outs = []
        for s in range(PROJ_CHUNK // LANES):
            a = acc[:, s * LANES:(s + 1) * LANES]
            swapped = jnp.where(first_half,
                                pltpu.roll(a, LANES - DIFF_QK_DIM // 2, 1),
                                pltpu.roll(a, DIFF_QK_DIM // 2, 1))
            outs.append(a * cos + swapped * sin)
        return jnp.concatenate(outs, axis=1)

    for c in range(ROW_WIDTH // PROJ_CHUNK):
        lo = c * PROJ_CHUNK
        lhs = h_perm if lo == SBK_CB * LANES else h
        acc = jnp.dot(lhs, w_ref[:, lo:lo + PROJ_CHUNK], preferred_element_type=F32)
        if lo == DK_CB * LANES:
            acc = rope(acc)
        o_ref[:, lo:lo + PROJ_CHUNK] = acc.astype(BF16)

    qt = _nt_dot(wt_ref[0:DIFF_QK_WIDTH, :], h)
    cos_f = cost_ref[...]
    sin_f = sint_ref[...]
    half = DIFF_QK_DIM // 2
    groups = []
    for gidx in range(DIFF_QK_WIDTH // DIFF_QK_DIM):
        xg = qt[gidx * DIFF_QK_DIM:(gidx + 1) * DIFF_QK_DIM]
        swapped = jnp.concatenate([xg[half:], xg[:half]], axis=0)
        groups.append((xg * cos_f + swapped * sin_f) * (QK_SCALE * LOG2E))
    ot_ref[0, 0:DIFF_QK_WIDTH, :] = jnp.concatenate(groups, axis=0).astype(BF16)
    ot_ref[0, DIFF_QK_WIDTH:T_WIDTH, :] = _nt_dot(wt_ref[DIFF_QK_WIDTH:T_WIDTH, :], h).astype(BF16)

    sq = (_nt_dot(wts_ref[0:SB_WIDTH, :], h) * (QK_SCALE * LOG2E)).astype(BF16)
    sv = _nt_dot(wts_ref[SB_WIDTH:TS_WIDTH, :], h_perm).astype(BF16)
    for blk in range(PROJ_ROWS // SB_BLOCK):
        cols = slice(blk * SB_BLOCK, (blk + 1) * SB_BLOCK)
        ots_ref[blk, 0:SB_WIDTH, :] = sq[:, cols]
        ots_ref[blk, SB_WIDTH:TS_WIDTH, :] = sv[:, cols]


def _in_projection(x2, g, w_rows, w_feat, w_feat_sb, rope, seq):
    n = x2.shape[0]
    pos_blocks = seq // PROJ_ROWS
    sb_per_step = PROJ_ROWS // SB_BLOCK
    cos_t, sin_t, cos_f, sin_f = rope
    const = lambda i: (0, 0)
    return pl.pallas_call(
        _inproj_kernel,
        grid=(n // PROJ_ROWS,),
        in_specs=[
            pl.BlockSpec((PROJ_ROWS, D_MODEL), lambda i: (i, 0)),
            pl.BlockSpec((1, D_MODEL), const),
            pl.BlockSpec((D_MODEL, ROW_WIDTH), const),
            pl.BlockSpec((T_WIDTH, D_MODEL), const),
            pl.BlockSpec((TS_WIDTH, D_MODEL), const),
            pl.BlockSpec((PROJ_ROWS, LANES), lambda i: (i % pos_blocks, 0)),
            pl.BlockSpec((PROJ_ROWS, LANES), lambda i: (i % pos_blocks, 0)),
            pl.BlockSpec((DIFF_QK_DIM, PROJ_ROWS), lambda i: (0, i % pos_blocks)),
            pl.BlockSpec((DIFF_QK_DIM, PROJ_ROWS), lambda i: (0, i % pos_blocks)),
        ],
        out_specs=[
            pl.BlockSpec((PROJ_ROWS, ROW_WIDTH), lambda i: (i, 0)),
            pl.BlockSpec((1, T_WIDTH, PROJ_ROWS), lambda i: (i, 0, 0)),
            pl.BlockSpec((sb_per_step, TS_WIDTH, SB_BLOCK), lambda i: (i, 0, 0)),
        ],
        out_shape=[
            jax.ShapeDtypeStruct((n, ROW_WIDTH), BF16),
            jax.ShapeDtypeStruct((n // PROJ_ROWS, T_WIDTH, PROJ_ROWS), BF16),
            jax.ShapeDtypeStruct((n // SB_BLOCK, TS_WIDTH, SB_BLOCK), BF16),
        ],
        compiler_params=pltpu.CompilerParams(
            dimension_semantics=("parallel",), vmem_limit_bytes=VMEM_LIMIT),
        name="in_projection",
    )(x2, g, w_rows, w_feat, w_feat_sb, cos_t, sin_t, cos_f, sin_f)


def _sb_kernel(qt_ref, k_ref, vt_ref, o_ref):
    i = pl.program_id(1)
    n_pairs = SB_WIDTH // LANES
    feat = lax.broadcasted_iota(jnp.int32, (LANES, SB_BLOCK), 0)
    pos = lax.broadcasted_iota(jnp.int32, (SB_BLOCK, SB_BLOCK), 0)
    qry = lax.broadcasted_iota(jnp.int32, (SB_BLOCK, SB_BLOCK), 1)
    key = (pos % 8) * SB_RUN + pos // 8
    strict = key < qry
    first_head = feat < HEAD_DIM

    def head_queries(p):
        qt = qt_ref[0, p * LANES:(p + 1) * LANES, :]
        zero = jnp.zeros_like(qt)
        return jnp.where(first_head, qt, zero), jnp.where(first_head, zero, qt)

    qts = [q for p in range(n_pairs) for q in head_queries(p)]

    def pair_lanes(head):
        return slice((head // 2) * LANES, (head // 2 + 1) * LANES)

    def scores(j, head):
        start = pl.multiple_of(j * SB_BLOCK, SB_BLOCK)
        return jnp.dot(k_ref[pl.ds(start, SB_BLOCK), pair_lanes(head)], qts[head],
                       preferred_element_type=F32)

    def block(j, head, z, run, masked):
        e = jnp.exp2(jnp.minimum(z, SB_MAX_LOG2))
        keep = 1.0 / (1.0 + e)
        beta = e * keep
        if masked:
            keep = jnp.where(strict, keep, 1.0)
            beta = jnp.where(strict, beta, 0.0)
        keep3 = keep.reshape(SB_RUN, 8, SB_BLOCK)
        beta3 = beta.reshape(SB_RUN, 8, SB_BLOCK)
        after = [None] * SB_RUN
        acc = jnp.ones((8, SB_BLOCK), F32)
        for v in range(SB_RUN - 1, -1, -1):
            after[v] = acc
            acc = acc * keep3[v]
        later = run
        offs = [None] * 8
        for r in range(7, -1, -1):
            offs[r] = later
            later = later * acc[r:r + 1]
        base = jnp.concatenate(offs, axis=0)
        a = jnp.concatenate([beta3[v] * after[v] * base for v in range(SB_RUN)], axis=0)
        pv = jnp.dot(vt_ref[j, pair_lanes(head), :], a.astype(BF16),
                     preferred_element_type=F32)
        return pv, later

    def pair_rows(pv_first, pv_second):
        return jnp.where(first_head, pv_first, pv_second)

    has_prev = i >= 1
    prev = jnp.maximum(i - 1, 0)
    z_diag = [scores(i, head) for head in range(SB_HEADS)]
    z_prev = [scores(prev, head) for head in range(SB_HEADS)]
    runs, pvs = [], []
    for head in range(SB_HEADS):
        pv_d, run_d = block(i, head, z_diag[head], jnp.ones((1, SB_BLOCK), F32), True)
        pv_p, run_p = block(prev, head, z_prev[head], run_d, False)
        pvs.append(pv_d + jnp.where(has_prev, pv_p, 0.0))
        runs.append(run_p)
    accs = [pair_rows(pvs[2 * p], pvs[2 * p + 1]) for p in range(n_pairs)]

    def alive(rs):
        top = rs[0]
        for r in rs[1:]:
            top = jnp.maximum(top, r)
        return (jnp.max(top) >= SB_MIN_WEIGHT).astype(jnp.int32)

    def cond(c):
        return (c[0] >= 0) & (c[1] > 0)

    def body(c):
        j, old_runs, old_accs = c[0], c[2:2 + SB_HEADS], c[2 + SB_HEADS:]
        new = [block(j, head, scores(j, head), old_runs[head], False)
               for head in range(SB_HEADS)]
        new_runs = [run for _, run in new]
        new_accs = [old_accs[p] + pair_rows(new[2 * p][0], new[2 * p + 1][0])
                    for p in range(n_pairs)]
        return (j - 1, alive(new_runs), *new_runs, *new_accs)

    out = lax.while_loop(cond, body, (i - 2, alive(runs), *runs, *accs))
    for p in range(n_pairs):
        o_ref[:, p * LANES:(p + 1) * LANES] = out[2 + SB_HEADS + p].T.astype(o_ref.dtype)


def _sb_attention(proj, proj_ts, batch, seq):
    nq = seq // SB_BLOCK
    q_rb, k_cb, v_rb = (SBQ_RB * LANES // SB_WIDTH, SBK_CB * LANES // SB_WIDTH,
                        SBV_RB * LANES // SB_WIDTH)
    return pl.pallas_call(
        _sb_kernel,
        grid=(batch, nq),
        in_specs=[
            pl.BlockSpec((1, SB_WIDTH, SB_BLOCK), lambda b, i: (b * nq + i, q_rb, 0)),
            pl.BlockSpec((seq, SB_WIDTH), lambda b, i: (b, k_cb)),
            pl.BlockSpec((nq, SB_WIDTH, SB_BLOCK), lambda b, i: (b, v_rb, 0)),
        ],
        out_specs=pl.BlockSpec((SB_BLOCK, SB_WIDTH), lambda b, i: (b * nq + i, 0)),
        out_shape=jax.ShapeDtypeStruct((batch * seq, SB_WIDTH), BF16),
        compiler_params=pltpu.CompilerParams(
            dimension_semantics=("parallel", "arbitrary"),
            vmem_limit_bytes=VMEM_LIMIT),
        name="stickbreak_attention",
    )(proj_ts, proj, proj_ts)


def _diff_kernel(lam_ref, gs_ref, qt_ref, k_ref, vt_ref, o_ref,
                 acc_ref, m_ref, l_ref, sa_ref, sb_ref, *, lambda_init):
    i = pl.program_id(1)
    feat = lax.broadcasted_iota(jnp.int32, (LANES, DIFF_BLOCK), 0)
    key = lax.broadcasted_iota(jnp.int32, (DIFF_BLOCK, DIFF_BLOCK), 0)
    qry = lax.broadcasted_iota(jnp.int32, (DIFF_BLOCK, DIFF_BLOCK), 1)
    causal = key <= qry
    first_comp = feat < DIFF_QK_DIM

    def head_queries(h):
        qt = qt_ref[0, h * LANES:(h + 1) * LANES, :]
        zero = jnp.zeros_like(qt)
        return jnp.where(first_comp, qt, zero), jnp.where(first_comp, zero, qt)

    qts = [q for h in range(DIFF_HEADS) for q in head_queries(h)]

    lv = lam_ref[...]
    lam = (jnp.exp(jnp.sum(lv[0:1] * lv[1:2], axis=-1, keepdims=True))
           - jnp.exp(jnp.sum(lv[2:3] * lv[3:4], axis=-1, keepdims=True)) + lambda_init)

    def head_lanes(h):
        return slice(h * LANES, (h + 1) * LANES)

    def scores(j, s_ref):
        start = pl.multiple_of(j * DIFF_BLOCK, DIFF_BLOCK)
        for h in range(DIFF_HEADS):
            kb = k_ref[pl.ds(start, DIFF_BLOCK), head_lanes(h)]
            for c in range(2):
                s_ref[2 * h + c] = jnp.dot(kb, qts[2 * h + c], preferred_element_type=F32)

    def absorb(j, s_ref, masked):
        for h in range(DIFF_HEADS):
            vt = vt_ref[j, head_lanes(h), :]
            for c in range(2):
                n = 2 * h + c
                m = m_ref[n]
                s = s_ref[n]
                if masked:
                    s = jnp.where(causal, s, -jnp.inf)
                m_new = jnp.maximum(m, jnp.max(s, axis=0, keepdims=True))
                alpha = jnp.exp2(m - m_new)
                p = jnp.exp2(s - m_new)
                m_ref[n] = m_new
                l_ref[n] = alpha * l_ref[n] + jnp.sum(p, axis=0, keepdims=True)
                acc_ref[n] = alpha * acc_ref[n] + jnp.dot(vt, p.astype(BF16),
                                                          preferred_element_type=F32)

    acc_ref[...] = jnp.zeros_like(acc_ref)
    l_ref[...] = jnp.zeros_like(l_ref)
    m_ref[...] = jnp.full(m_ref.shape, -jnp.inf, F32)

    scores(0, sa_ref)

    def pair(t, _):
        b = 2 * t
        scores(b + 1, sb_ref)
        absorb(b, sa_ref, False)
        scores(b + 2, sa_ref)
        absorb(b + 1, sb_ref, False)
        return 0

    lax.fori_loop(0, i // 2, pair, 0)

    @pl.when(i % 2 == 0)
    def _():
        absorb(i, sa_ref, True)

    @pl.when(i % 2 == 1)
    def _():
        scores(i, sb_ref)
        absorb(i - 1, sa_ref, False)
        absorb(i, sb_ref, True)

    for h in range(DIFF_HEADS):
        ot = (acc_ref[2 * h] / l_ref[2 * h]
              - lam * (acc_ref[2 * h + 1] / l_ref[2 * h + 1]))
        ms = jnp.mean(ot * ot, axis=0, keepdims=True)
        ot = ot * lax.rsqrt(ms + EPS) * gs_ref[...] * (1.0 - lambda_init)
        o_ref[:, head_lanes(h)] = ot.T.astype(o_ref.dtype)


def _diff_attention(proj, proj_t, lam_rows, g_subln_col, batch, seq, lambda_init):
    nq = seq // DIFF_BLOCK
    n_chains = 2 * DIFF_HEADS
    q_rb, k_cb, v_rb = (DQ_RB * LANES // DIFF_V_WIDTH, DK_CB * LANES // DIFF_V_WIDTH,
                        DV_RB * LANES // DIFF_V_WIDTH)
    return pl.pallas_call(
        functools.partial(_diff_kernel, lambda_init=lambda_init),
        grid=(batch, nq),
        in_specs=[
            pl.BlockSpec((4, DIFF_QK_DIM), lambda b, i: (0, 0)),
            pl.BlockSpec((DIFF_V_DIM, 1), lambda b, i: (0, 0)),
            pl.BlockSpec((1, DIFF_QK_WIDTH, DIFF_BLOCK), lambda b, i: (b * nq + i, q_rb, 0)),
            pl.BlockSpec((seq, DIFF_QK_WIDTH), lambda b, i: (b, k_cb)),
            pl.BlockSpec((nq, DIFF_V_WIDTH, DIFF_BLOCK), lambda b, i: (b, v_rb, 0)),
        ],
        out_specs=pl.BlockSpec((DIFF_BLOCK, DIFF_V_WIDTH), lambda b, i: (b * nq + i, 0)),
        out_shape=jax.ShapeDtypeStruct((batch * seq, DIFF_V_WIDTH), BF16),
        scratch_shapes=[pltpu.VMEM((n_chains, DIFF_V_DIM, DIFF_BLOCK), F32),
                        pltpu.VMEM((n_chains, 1, DIFF_BLOCK), F32),
                        pltpu.VMEM((n_chains, 1, DIFF_BLOCK), F32),
                        pltpu.VMEM((n_chains, DIFF_BLOCK, DIFF_BLOCK), F32),
                        pltpu.VMEM((n_chains, DIFF_BLOCK, DIFF_BLOCK), F32)],
        compiler_params=pltpu.CompilerParams(
            dimension_semantics=("parallel", "arbitrary"),
            vmem_limit_bytes=VMEM_LIMIT),
        name="differential_attention",
    )(lam_rows, g_subln_col, proj_t, proj, proj_t)


def _post_kernel(oa_ref, ob_ref, ga_ref, gb_ref, x_ref, wua_ref, wub_ref, wo_ref, gn_ref,
                 wr_hi_ref, wr_lo_ref, br_ref,
                 x1_ref, h2_ref, slab_ref, cnt_ref, carry_ref):
    step_id = pl.program_id(0)

    @pl.when(step_id == 0)
    def _():
        carry_ref[...] = jnp.zeros_like(carry_ref)

    u_a = jnp.dot(oa_ref[...], wua_ref[...], preferred_element_type=F32)
    u_b = jnp.dot(ob_ref[...], wub_ref[...], preferred_element_type=F32)
    y = (jax.nn.sigmoid(ga_ref[...].astype(F32)) * u_a
         + jax.nn.sigmoid(gb_ref[...].astype(F32)) * u_b)
    x1 = x_ref[...] + jnp.dot(y.astype(BF16), wo_ref[...], preferred_element_type=F32)
    x1_ref[...] = x1
    ms = jnp.mean(x1 * x1, axis=-1, keepdims=True)
    h2 = x1 * lax.rsqrt(ms + EPS) * gn_ref[...]
    _store_row_tiles(h2_ref, h2)

    h_hi = h2.astype(BF16)
    h_lo = (h2 - h_hi.astype(F32)).astype(BF16)
    logits = (jnp.dot(h_hi, wr_hi_ref[...], preferred_element_type=F32)
              + jnp.dot(h_lo, wr_hi_ref[...], preferred_element_type=F32)
              + jnp.dot(h_hi, wr_lo_ref[...], preferred_element_type=F32)
              + br_ref[...])

    lane = lax.broadcasted_iota(jnp.int32, (POST_ROWS, LANES), 1)
    neg = -jnp.inf

    def first_argmax(v):
        m = jnp.max(v, axis=-1, keepdims=True)
        idx = jnp.min(jnp.where(v == m, lane, LANES), axis=-1, keepdims=True)
        return m, idx

    g_logits = jnp.where(lane < N_GROUPS, logits, neg)
    g_max, grp = first_argmax(g_logits)
    p_grp = 1.0 / jnp.sum(jnp.exp(g_logits - g_max), axis=-1, keepdims=True)

    lo_lane = ROUTER_EXPERT_LANE0 + grp * EXPERTS_PER_GROUP
    in_grp = (lane >= lo_lane) & (lane < lo_lane + EXPERTS_PER_GROUP)
    e_logits = jnp.where(in_grp, logits, neg)
    m1, i1 = first_argmax(e_logits)
    m2, i2 = first_argmax(jnp.where(lane == i1, neg, e_logits))
    r = jnp.exp(m2 - m1)
    w0 = p_grp / (1.0 + r)
    w1 = p_grp * r / (1.0 + r)

    hit0 = lane == i1
    hit1 = lane == i2
    onehot = jnp.where(hit0 | hit1, 1.0, 0.0)
    trow = lax.broadcasted_iota(jnp.int32, (POST_ROWS, POST_ROWS), 0)
    tcol = lax.broadcasted_iota(jnp.int32, (POST_ROWS, POST_ROWS), 1)
    earlier = jnp.where(tcol < trow, 1.0, 0.0).astype(BF16)
    before = carry_ref[0:1, :] + jnp.dot(earlier, onehot.astype(BF16),
                                         preferred_element_type=F32)
    r0 = jnp.sum(jnp.where(hit0, before, 0.0), axis=-1, keepdims=True)
    r1 = jnp.sum(jnp.where(hit1, before, 0.0), axis=-1, keepdims=True)
    new_carry = carry_ref[0:1, :] + jnp.sum(onehot, axis=0, keepdims=True)
    carry_ref[0:1, :] = new_carry
    cnt_ref[...] = jnp.broadcast_to(new_carry, cnt_ref.shape)

    e0 = (i1 - ROUTER_EXPERT_LANE0).astype(F32)
    e1 = (i2 - ROUTER_EXPERT_LANE0).astype(F32)
    slab = jnp.zeros((POST_ROWS, LANES), F32)
    for pos, val in ((SLAB_E0, e0), (SLAB_E1, e1), (SLAB_R0, r0), (SLAB_R1, r1),
                     (SLAB_W0, w0), (SLAB_W1, w1)):
        slab = jnp.where(lane == pos, val, slab)
    slab_ref[...] = slab


def _post_attention(o_a, o_b, proj, x2, wua, wub, wo, gn, wr_hi, wr_lo, br):
    n = x2.shape[0]
    const = lambda i: (0, 0)
    return pl.pallas_call(
        _post_kernel,
        grid=(n // POST_ROWS,),
        in_specs=[
            pl.BlockSpec((POST_ROWS, SB_WIDTH), lambda i: (i, 0)),
            pl.BlockSpec((POST_ROWS, DIFF_V_WIDTH), lambda i: (i, 0)),
            pl.BlockSpec((POST_ROWS, D_MODEL), lambda i: (i, GATE_A_B1024)),
            pl.BlockSpec((POST_ROWS, D_MODEL), lambda i: (i, GATE_B_B1024)),
            pl.BlockSpec((POST_ROWS, D_MODEL), lambda i: (i, 0)),
            pl.BlockSpec((SB_WIDTH, D_MODEL), const),
            pl.BlockSpec((DIFF_V_WIDTH, D_MODEL), const),
            pl.BlockSpec((D_MODEL, D_MODEL), const),
            pl.BlockSpec((1, D_MODEL), const),
            pl.BlockSpec((D_MODEL, LANES), const),
            pl.BlockSpec((D_MODEL, LANES), const),
            pl.BlockSpec((1, LANES), const),
        ],
        out_specs=[
            pl.BlockSpec((POST_ROWS, D_MODEL), lambda i: (i, 0)),
            pl.BlockSpec((POST_ROWS * ROW_TILE, LANES), lambda i: (i, 0)),
            pl.BlockSpec((POST_ROWS, LANES), lambda i: (i, 0)),
            pl.BlockSpec((8, LANES), const),
        ],
        out_shape=[
            jax.ShapeDtypeStruct((n, D_MODEL), F32),
            jax.ShapeDtypeStruct((n * ROW_TILE, LANES), F32),
            jax.ShapeDtypeStruct((n, LANES), F32),
            jax.ShapeDtypeStruct((8, LANES), F32),
        ],
        scratch_shapes=[pltpu.VMEM((8, LANES), F32)],
        compiler_params=pltpu.CompilerParams(
            dimension_semantics=("arbitrary",), vmem_limit_bytes=VMEM_LIMIT),
        name="merge_route",
    )(o_a, o_b, proj, proj, x2, wua, wub, wo, gn, wr_hi, wr_lo, br)


def _issue_row_copies(make_copy):
    def group(g, _):
        for u in range(ROW_DMA_UNROLL):
            for k in range(TOP_K):
                make_copy(g * ROW_DMA_UNROLL + u, k).start(priority=k % 2)
        return 0

    lax.fori_loop(0, ROW_DMA_TOKENS // ROW_DMA_UNROLL, group, 0)


def _scatter_kernel(dest_ref, pad_ref, h_ref, xd_ref, zero_tile, sem):
    n_pad = pad_ref.shape[-1]
    zero_tile[...] = jnp.zeros_like(zero_tile)
    _issue_row_copies(
        lambda t, k: _row_tile_copy(h_ref, t, xd_ref, dest_ref[0, 0, t * TOP_K + k], sem))
    for z in range(n_pad):
        _row_tile_copy(zero_tile, 0, xd_ref, pad_ref[0, 0, z], sem).start(priority=z % 2)
    for _ in range(TOP_K):
        pltpu.make_async_copy(
            h_ref, xd_ref.at[pl.ds(0, ROW_DMA_TOKENS * ROW_TILE), :], sem).wait()
    pltpu.make_async_copy(
        h_ref.at[pl.ds(0, n_pad * ROW_TILE), :], xd_ref.at[pl.ds(0, n_pad * ROW_TILE), :],
        sem).wait()


def _scatter_rows(dest3, pad3, h2_tiles, n_rows):
    n = h2_tiles.shape[0] // ROW_TILE
    steps = n // ROW_DMA_TOKENS
    assert pad3.shape[0] == steps and pad3.shape[2] <= ROW_DMA_TOKENS
    return pl.pallas_call(
        _scatter_kernel,
        grid=(steps,),
        in_specs=[
            pl.BlockSpec((1, 1, TOP_K * ROW_DMA_TOKENS), lambda i: (i, 0, 0),
                         memory_space=pltpu.SMEM),
            pl.BlockSpec((1, 1, pad3.shape[2]), lambda i: (i, 0, 0), memory_space=pltpu.SMEM),
            pl.BlockSpec((ROW_DMA_TOKENS * ROW_TILE, LANES), lambda i: (i, 0)),
        ],
        out_specs=pl.BlockSpec(memory_space=pl.ANY),
        out_shape=jax.ShapeDtypeStruct((n_rows * ROW_TILE, LANES), F32),
        scratch_shapes=[pltpu.VMEM((ROW_TILE, LANES), F32), pltpu.SemaphoreType.DMA],
        compiler_params=pltpu.CompilerParams(
            dimension_semantics=("arbitrary",), vmem_limit_bytes=VMEM_LIMIT),
        name="scatter_rows",
    )(dest3, pad3, h2_tiles)


def _expert_kernel(be_ref, nb_ref, x_ref, *refs):
    w_refs, y_ref = refs[:-1], refs[-1]
    first_block = pl.program_id(0) * MOE_GROUP

    @pl.when(first_block < nb_ref[0])
    def _():
        for blk in range(MOE_GROUP):
            wg_ref, wu_ref, wd_ref = w_refs[3 * blk:3 * blk + 3]
            x = _load_row_tiles(x_ref, MOE_BLOCK, first_row=blk * MOE_BLOCK).astype(BF16)
            g = jnp.dot(x, wg_ref[0], preferred_element_type=F32)
            u = jnp.dot(x, wu_ref[0], preferred_element_type=F32)
            hmid = (g * jax.nn.sigmoid(g) * u).astype(BF16)
            _store_row_tiles(y_ref, jnp.dot(hmid, wd_ref[0], preferred_element_type=F32),
                             first_row=blk * MOE_BLOCK)

    @pl.when(first_block >= nb_ref[0])
    def _():
        y_ref[...] = jnp.zeros_like(y_ref)


def _expert_mlp(block_e, n_used, xd_tiles, wg, wu, wd):
    n_blocks = xd_tiles.shape[0] // (MOE_BLOCK * ROW_TILE)
    assert n_blocks % MOE_GROUP == 0
    step_rows = MOE_GROUP * MOE_BLOCK * ROW_TILE

    def w_spec(shape, blk):
        return pl.BlockSpec((1,) + shape, lambda i, be, nb: (be[i * MOE_GROUP + blk], 0, 0))

    w_specs, w_args = [], []
    for blk in range(MOE_GROUP):
        w_specs += [w_spec((D_MODEL, D_EXPERT), blk), w_spec((D_MODEL, D_EXPERT), blk),
                    w_spec((D_EXPERT, D_MODEL), blk)]
        w_args += [wg, wu, wd]
    return pl.pallas_call(
        _expert_kernel,
        grid_spec=pltpu.PrefetchScalarGridSpec(
            num_scalar_prefetch=2,
            grid=(n_blocks // MOE_GROUP,),
            in_specs=[pl.BlockSpec((step_rows, LANES), lambda i, be, nb: (i, 0))] + w_specs,
            out_specs=pl.BlockSpec((step_rows, LANES), lambda i, be, nb: (i, 0)),
        ),
        out_shape=jax.ShapeDtypeStruct(xd_tiles.shape, F32),
        compiler_params=pltpu.CompilerParams(
            dimension_semantics=("arbitrary",), vmem_limit_bytes=VMEM_LIMIT),
        name="expert_mlp",
    )(block_e, n_used, xd_tiles, *w_args)


def _combine_kernel(dest_ref, next_dest_ref, x1_ref, slab_ref, g_ref, yd_ref, o_ref, ybuf, sems):
    step = pl.program_id(0)
    slot_rows = TOP_K * ROW_DMA_TOKENS

    def gather(idx_ref, slot):
        _issue_row_copies(
            lambda t, k: _row_tile_copy(yd_ref, idx_ref[0, 0, t * TOP_K + k], ybuf,
                                        slot * slot_rows + k * ROW_DMA_TOKENS + t,
                                        sems.at[slot]))

    @pl.when(step == 0)
    def _():
        gather(dest_ref, 0)

    @pl.when(step + 1 < pl.num_programs(0))
    def _():
        gather(next_dest_ref, (step + 1) % 2)

    slot = step % 2
    first = slot * slot_rows
    pltpu.make_async_copy(
        yd_ref.at[pl.ds(0, slot_rows * ROW_TILE), :],
        ybuf.at[pl.ds(pl.multiple_of(first * ROW_TILE, ROW_TILE), slot_rows * ROW_TILE), :],
        sems.at[slot]).wait()

    slab = slab_ref[...]
    w0 = slab[:, SLAB_W0:SLAB_W0 + 1]
    w1 = slab[:, SLAB_W1:SLAB_W1 + 1]
    x = (x1_ref[...] + w0 * _load_row_tiles(ybuf, ROW_DMA_TOKENS, first_row=first)
         + w1 * _load_row_tiles(ybuf, ROW_DMA_TOKENS, first_row=first + ROW_DMA_TOKENS))
    ms = jnp.mean(x * x, axis=-1, keepdims=True)
    o_ref[...] = x * lax.rsqrt(ms + EPS) * g_ref[...]


def _combine(dest3, x1, slab, g_final, yd_tiles):
    n = x1.shape[0]
    steps = n // ROW_DMA_TOKENS
    return pl.pallas_call(
        _combine_kernel,
        grid=(steps,),
        in_specs=[
            pl.BlockSpec((1, 1, TOP_K * ROW_DMA_TOKENS), lambda i: (i, 0, 0),
                         memory_space=pltpu.SMEM),
            pl.BlockSpec((1, 1, TOP_K * ROW_DMA_TOKENS),
                         lambda i: (jnp.minimum(i + 1, steps - 1), 0, 0),
                         memory_space=pltpu.SMEM),
            pl.BlockSpec((ROW_DMA_TOKENS, D_MODEL), lambda i: (i, 0)),
            pl.BlockSpec((ROW_DMA_TOKENS, LANES), lambda i: (i, 0)),
            pl.BlockSpec((1, D_MODEL), lambda i: (0, 0)),
            pl.BlockSpec(memory_space=pl.ANY),
        ],
        out_specs=pl.BlockSpec((ROW_DMA_TOKENS, D_MODEL), lambda i: (i, 0)),
        out_shape=jax.ShapeDtypeStruct((n, D_MODEL), F32),
        scratch_shapes=[pltpu.VMEM((2 * TOP_K * ROW_DMA_TOKENS * ROW_TILE, LANES), F32),
                        pltpu.SemaphoreType.DMA((2,))],
        compiler_params=pltpu.CompilerParams(
            dimension_semantics=("arbitrary",), vmem_limit_bytes=VMEM_LIMIT),
        name="combine_norm",
    )(dest3, dest3, x1, slab, g_final, yd_tiles)


def _rope_tables(seq):
    pos = jnp.arange(seq, dtype=F32)
    inv_freq = 1.0 / (ROPE_THETA ** (jnp.arange(0, DIFF_QK_DIM, 2, dtype=F32) / DIFF_QK_DIM))
    freqs = pos[:, None] * inv_freq[None, :]
    emb = jnp.concatenate([freqs, freqs], axis=-1)
    cos, sin = jnp.cos(emb), jnp.sin(emb)
    half = DIFF_QK_DIM // 2
    sin_signed = jnp.concatenate([-sin[:, :half], sin[:, half:]], axis=-1)
    reps = LANES // DIFF_QK_DIM
    return (jnp.tile(cos, (1, reps)), jnp.tile(sin_signed, (1, reps)), cos.T, sin_signed.T)


def _split_in_projection(w):
    edges = [0]
    for width in (SB_WIDTH, SB_WIDTH, SB_WIDTH, DIFF_QK_WIDTH, DIFF_QK_WIDTH, DIFF_V_WIDTH,
                  D_MODEL, D_MODEL):
        edges.append(edges[-1] + width)
    sb_q, sb_k, sb_v, d_q, d_k, d_v, gate_a, gate_b = (
        w[:, lo:hi] for lo, hi in zip(edges[:-1], edges[1:]))
    w_rows = jnp.concatenate([sb_k, d_k, gate_a, gate_b], axis=1)
    w_feat = jnp.concatenate([d_q, d_v], axis=1).T
    w_feat_sb = jnp.concatenate([sb_q, sb_v], axis=1).T
    return w_rows.astype(BF16), w_feat.astype(BF16), w_feat_sb.astype(BF16)


def _split_bf16(w):
    hi = w.astype(BF16)
    lo = (w - hi.astype(F32)).astype(BF16)
    return hi, lo


def kernel(x, g_norm_mix, w_in, lambda_q1, lambda_k1, lambda_q2, lambda_k2, g_subln,
           w_up_a, w_up_b, w_out, g_norm_ffn, w_router_group, b_router_group,
           w_router_expert, b_router_expert, w_expert_gate, w_expert_up, w_expert_down,
           g_norm_final):
    batch, seq, d = x.shape
    depth = w_in.shape[0]
    n_tok = batch * seq
    assert depth == 1, "the final RMSNorm is fused into the layer's combine step"
    assert d == D_MODEL and seq % DIFF_BLOCK == 0 and seq % PROJ_ROWS == 0
    assert seq % SB_BLOCK == 0 and PROJ_ROWS % SB_BLOCK == 0
    assert DIFF_BLOCK == PROJ_ROWS, "feature-major q/v blocks are written per projection step"
    assert (N_EXPERTS * MOE_BLOCK) % (n_tok // ROW_DMA_TOKENS) == 0
    assert n_tok % POST_ROWS == 0 and n_tok % ROW_DMA_TOKENS == 0
    n_assign = n_tok * TOP_K
    n_blocks = n_assign // MOE_BLOCK + N_EXPERTS
    n_rows = n_blocks * MOE_BLOCK

    rope = _rope_tables(seq)
    x2 = x.reshape(n_tok, d)
    for l in range(depth):
        lambda_init = 0.8 - 0.6 * math.exp(-0.3 * l)
        w_rows, w_feat, w_feat_sb = _split_in_projection(w_in[l])
        proj, proj_t, proj_ts = _in_projection(x2, g_norm_mix[l][None, :], w_rows, w_feat,
                                               w_feat_sb, rope, seq)
        o_a = _sb_attention(proj, proj_ts, batch, seq)
        lam_rows = jnp.stack([lambda_q1[l], lambda_k1[l], lambda_q2[l], lambda_k2[l]])
        o_b = _diff_attention(proj, proj_t, lam_rows, g_subln[l][:, None], batch, seq,
                              lambda_init)

        w_r = jnp.zeros((d, LANES), F32)
        w_r = w_r.at[:, :N_GROUPS].set(w_router_group[l])
        w_r = w_r.at[:, N_GROUPS:N_GROUPS + N_EXPERTS].set(w_router_expert[l])
        b_r = jnp.zeros((1, LANES), F32)
        b_r = b_r.at[0, :N_GROUPS].set(b_router_group[l])
        b_r = b_r.at[0, N_GROUPS:N_GROUPS + N_EXPERTS].set(b_router_expert[l])
        wr_hi, wr_lo = _split_bf16(w_r)

        x1, h2, slab, cnt = _post_attention(
            o_a, o_b, proj, x2, w_up_a[l].astype(BF16), w_up_b[l].astype(BF16),
            w_out[l].astype(BF16), g_norm_ffn[l][None, :], wr_hi, wr_lo, b_r)

        counts = cnt[0, N_GROUPS:N_GROUPS + N_EXPERTS].astype(jnp.int32)
        padded = (counts + MOE_BLOCK - 1) // MOE_BLOCK * MOE_BLOCK
        pad_end = jnp.cumsum(padded)
        pad_start = pad_end - padded
        e_ids = slab[:, SLAB_E0:SLAB_E1 + 1].astype(jnp.int32)
        ranks = slab[:, SLAB_R0:SLAB_R1 + 1].astype(jnp.int32)
        expert_iota = jnp.arange(N_EXPERTS, dtype=jnp.int32)
        dest = ranks + jnp.sum(
            jnp.where(e_ids[:, :, None] == expert_iota, pad_start, 0), axis=-1)
        steps = n_tok // ROW_DMA_TOKENS
        dest3 = dest.reshape(steps, 1, TOP_K * ROW_DMA_TOKENS)
        gap = padded - counts
        gap_end = jnp.cumsum(gap)
        z = jnp.arange(n_rows - n_assign, dtype=jnp.int32)
        seg = jnp.sum((gap_end[None, :] <= z[:, None]).astype(jnp.int32), axis=1)
        seg_onehot = seg[:, None] == jnp.arange(N_EXPERTS + 1, dtype=jnp.int32)
        seg_first_row = jnp.concatenate([pad_start + counts, pad_end[-1:]])
        seg_first_z = jnp.concatenate([gap_end - gap, gap_end[-1:]])
        pad_rows = z + jnp.sum(jnp.where(seg_onehot, seg_first_row - seg_first_z, 0), axis=1)
        pad3 = pad_rows.reshape(steps, 1, (n_rows - n_assign) // steps)
        block_start = jnp.arange(n_blocks, dtype=jnp.int32) * MOE_BLOCK
        block_e = jnp.minimum(
            jnp.sum((pad_end[None, :] <= block_start[:, None]).astype(jnp.int32), axis=1),
            N_EXPERTS - 1)
        n_used = (pad_end[-1:] // MOE_BLOCK).astype(jnp.int32)

        xd = _scatter_rows(dest3, pad3, h2, n_rows)
        yd = _expert_mlp(block_e, n_used, xd, w_expert_gate[l].astype(BF16),
                         w_expert_up[l].astype(BF16), w_expert_down[l].astype(BF16))
        x2 = _combine(dest3, x1, slab, g_norm_final[None, :], yd)
    return x2.reshape(batch, seq, d)
```

```python
import functools
import math

import jax
import jax.numpy as jnp
from jax import lax
from jax.experimental import pallas as pl
from jax.experimental.pallas import tpu as pltpu

D_MODEL = 1024
HEAD_DIM = 64
SB_HEADS = 8
SB_WIDTH = SB_HEADS * HEAD_DIM
DIFF_HEADS = 4
DIFF_QK_DIM = 64
DIFF_V_DIM = 2 * DIFF_QK_DIM
DIFF_QK_WIDTH = DIFF_HEADS * 2 * DIFF_QK_DIM
DIFF_V_WIDTH = DIFF_HEADS * DIFF_V_DIM
IN_WIDTH = 3 * SB_WIDTH + 2 * DIFF_QK_WIDTH + DIFF_V_WIDTH + 2 * D_MODEL
ROPE_THETA = 10000.0
N_GROUPS = 4
EXPERTS_PER_GROUP = 8
N_EXPERTS = N_GROUPS * EXPERTS_PER_GROUP
TOP_K = 2
D_EXPERT = 256
EPS = 1e-6

LANES = 128
SUBLANES = 8
QK_SCALE = HEAD_DIM ** -0.5
LOG2E = math.log2(math.e)

F32 = jnp.float32
BF16 = jnp.bfloat16

PROJ_ROWS = 512
PROJ_CHUNK = 512
SB_BLOCK = 256
DIFF_BLOCK = 512
SB_MIN_WEIGHT = 2.0 ** -126
SB_MAX_LOG2 = 126.0
POST_ROWS = 512
MOE_BLOCK = 256
MOE_GROUP = 4
ROW_DMA_TOKENS = 512
ROW_DMA_UNROLL = 8
VMEM_LIMIT = 56 * 1024 * 1024

SB_RUN = SB_BLOCK // SUBLANES
ROW_WIDTH = SB_WIDTH + DIFF_QK_WIDTH + 2 * D_MODEL
T_WIDTH = DIFF_QK_WIDTH + DIFF_V_WIDTH
TS_WIDTH = 2 * SB_WIDTH
SBK_CB, DK_CB = 0, 4
GATE_A_B1024, GATE_B_B1024 = 1, 2
DQ_RB, DV_RB = 0, DIFF_QK_WIDTH // LANES
SBQ_RB, SBV_RB = 0, SB_WIDTH // LANES

SLAB_E0, SLAB_E1, SLAB_R0, SLAB_R1, SLAB_W0, SLAB_W1 = 0, 1, 2, 3, 4, 5
ROUTER_EXPERT_LANE0 = N_GROUPS


def _nt_dot(a, b):
    return lax.dot_general(a, b, (((1,), (1,)), ((), ())), preferred_element_type=F32)


def _tnt_dot(a, b):
    return lax.dot_general(a, b, (((0,), (1,)), ((), ())), preferred_element_type=F32)


ROW_TILE = D_MODEL // LANES


def _store_row_tiles(ref, value, first_row=0):
    rows = value.shape[0]
    for k in range(ROW_TILE):
        ref[pl.ds(first_row * ROW_TILE + k, rows, stride=ROW_TILE), :] = (
            value[:, k * LANES:(k + 1) * LANES])


def _load_row_tiles(ref, rows, first_row=0):
    return jnp.concatenate(
        [ref[pl.ds(first_row * ROW_TILE + k, rows, stride=ROW_TILE), :]
         for k in range(ROW_TILE)], axis=1)


def _row_tile_copy(src_ref, src_row, dst_ref, dst_row, sem):
    src = src_ref.at[pl.ds(pl.multiple_of(src_row * ROW_TILE, ROW_TILE), ROW_TILE), :]
    dst = dst_ref.at[pl.ds(pl.multiple_of(dst_row * ROW_TILE, ROW_TILE), ROW_TILE), :]
    return pltpu.make_async_copy(src, dst, sem)


def _inproj_kernel(x_ref, g_ref, w_ref, wt_ref, wts_ref, cos_ref, sin_ref, cost_ref, sint_ref,
                   o_ref, ot_ref, ots_ref):
    def normed(x):
        ms = jnp.mean(x * x, axis=-1, keepdims=True)
        return (x * lax.rsqrt(ms + EPS) * g_ref[...]).astype(BF16)

    h = normed(x_ref[...])
    dst = lax.broadcasted_iota(jnp.int32, (SB_BLOCK, SB_BLOCK), 0)
    src = lax.broadcasted_iota(jnp.int32, (SB_BLOCK, SB_BLOCK), 1)
    perm = jnp.where(src == (dst % SUBLANES) * SB_RUN + dst // SUBLANES, 1.0, 0.0).astype(BF16)
    h_perm = jnp.concatenate(
        [jnp.dot(perm, h[base:base + SB_BLOCK], preferred_element_type=F32).astype(BF16)
         for base in range(0, PROJ_ROWS, SB_BLOCK)], axis=0)

    lane = lax.broadcasted_iota(jnp.int32, (PROJ_ROWS, LANES), 1)
    first_half = (lane % DIFF_QK_DIM) < (DIFF_QK_DIM // 2)
    cos = cos_ref[...]
    sin = sin_ref[...]

    def rope(acc):
        outs = []
        for s in range(PROJ_CHUNK // LANES):
            a = acc[:, s * LANES:(s + 1) * LANES]
            swapped = jnp.where(first_half,
                                pltpu.roll(a, LANES - DIFF_QK_DIM // 2, 1),
                                pltpu.roll(a, DIFF_QK_DIM // 2, 1))
            outs.append(a * cos + swapped * sin)
        return jnp.concatenate(outs, axis=1)

    for c in range(ROW_WIDTH // PROJ_CHUNK):
        lo = c * PROJ_CHUNK
        lhs = h_perm if lo == SBK_CB * LANES else h
        acc = jnp.dot(lhs, w_ref[:, lo:lo + PROJ_CHUNK], preferred_element_type=F32)
        if lo == DK_CB * LANES:
            acc = rope(acc)
        o_ref[:, lo:lo + PROJ_CHUNK] = acc.astype(BF16)

    qt = _tnt_dot(wt_ref[:, 0:DIFF_QK_WIDTH], h)
    cos_f = cost_ref[...]
    sin_f = sint_ref[...]
    half = DIFF_QK_DIM // 2
    groups = []
    for gidx in range(DIFF_QK_WIDTH // DIFF_QK_DIM):
        xg = qt[gidx * DIFF_QK_DIM:(gidx + 1) * DIFF_QK_DIM]
        swapped = jnp.concatenate([xg[half:], xg[:half]], axis=0)
        groups.append((xg * cos_f + swapped * sin_f) * (QK_SCALE * LOG2E))
    ot_ref[0, 0:DIFF_QK_WIDTH, :] = jnp.concatenate(groups, axis=0).astype(BF16)
    ot_ref[0, DIFF_QK_WIDTH:T_WIDTH, :] = _tnt_dot(wt_ref[:, DIFF_QK_WIDTH:T_WIDTH],
                                                   h).astype(BF16)

    sq = (_tnt_dot(wts_ref[:, 0:SB_WIDTH], h) * (QK_SCALE * LOG2E)).astype(BF16)
    sv = _tnt_dot(wts_ref[:, SB_WIDTH:TS_WIDTH], h_perm).astype(BF16)
    for blk in range(PROJ_ROWS // SB_BLOCK):
        cols = slice(blk * SB_BLOCK, (blk + 1) * SB_BLOCK)
        ots_ref[blk, 0:SB_WIDTH, :] = sq[:, cols]
        ots_ref[blk, SB_WIDTH:TS_WIDTH, :] = sv[:, cols]


def _in_projection(x2, g, w_rows, w_feat, w_feat_sb, rope, seq):
    n = x2.shape[0]
    pos_blocks = seq // PROJ_ROWS
    sb_per_step = PROJ_ROWS // SB_BLOCK
    cos_t, sin_t, cos_f, sin_f = rope
    const = lambda i: (0, 0)
    return pl.pallas_call(
        _inproj_kernel,
        grid=(n // PROJ_ROWS,),
        in_specs=[
            pl.BlockSpec((PROJ_ROWS, D_MODEL), lambda i: (i, 0)),
            pl.BlockSpec((1, D_MODEL), const),
            pl.BlockSpec((D_MODEL, ROW_WIDTH), const),
            pl.BlockSpec((D_MODEL, T_WIDTH), const),
            pl.BlockSpec((D_MODEL, TS_WIDTH), const),
            pl.BlockSpec((PROJ_ROWS, LANES), lambda i: (i % pos_blocks, 0)),
            pl.BlockSpec((PROJ_ROWS, LANES), lambda i: (i % pos_blocks, 0)),
            pl.BlockSpec((DIFF_QK_DIM, PROJ_ROWS), lambda i: (0, i % pos_blocks)),
            pl.BlockSpec((DIFF_QK_DIM, PROJ_ROWS), lambda i: (0, i % pos_blocks)),
        ],
        out_specs=[
            pl.BlockSpec((PROJ_ROWS, ROW_WIDTH), lambda i: (i, 0)),
            pl.BlockSpec((1, T_WIDTH, PROJ_ROWS), lambda i: (i, 0, 0)),
            pl.BlockSpec((sb_per_step, TS_WIDTH, SB_BLOCK), lambda i: (i, 0, 0)),
        ],
        out_shape=[
            jax.ShapeDtypeStruct((n, ROW_WIDTH), BF16),
            jax.ShapeDtypeStruct((n // PROJ_ROWS, T_WIDTH, PROJ_ROWS), BF16),
            jax.ShapeDtypeStruct((n // SB_BLOCK, TS_WIDTH, SB_BLOCK), BF16),
        ],
        compiler_params=pltpu.CompilerParams(
            dimension_semantics=("parallel",), vmem_limit_bytes=VMEM_LIMIT),
        name="in_projection",
    )(x2, g, w_rows, w_feat, w_feat_sb, cos_t, sin_t, cos_f, sin_f)


def _sb_kernel(qt_ref, k_ref, vt_ref, o_ref):
    i = pl.program_id(1)
    n_pairs = SB_WIDTH // LANES
    feat = lax.broadcasted_iota(jnp.int32, (LANES, SB_BLOCK), 0)
    pos = lax.broadcasted_iota(jnp.int32, (SB_BLOCK, SB_BLOCK), 0)
    qry = lax.broadcasted_iota(jnp.int32, (SB_BLOCK, SB_BLOCK), 1)
    key = (pos % SUBLANES) * SB_RUN + pos // SUBLANES
    strict = key < qry
    first_head = feat < HEAD_DIM

    def head_queries(p):
        qt = qt_ref[0, p * LANES:(p + 1) * LANES, :]
        zero = jnp.zeros_like(qt)
        return jnp.where(first_head, qt, zero), jnp.where(first_head, zero, qt)

    qts = [q for p in range(n_pairs) for q in head_queries(p)]

    def pair_lanes(head):
        return slice((head // 2) * LANES, (head // 2 + 1) * LANES)

    def scores(j, head):
        start = pl.multiple_of(j * SB_BLOCK, SB_BLOCK)
        return jnp.dot(k_ref[pl.ds(start, SB_BLOCK), pair_lanes(head)], qts[head],
                       preferred_element_type=F32)

    def block(j, head, z, run, masked):
        e = jnp.exp2(jnp.minimum(z, SB_MAX_LOG2))
        keep = 1.0 / (1.0 + e)
        beta = e * keep
        if masked:
            keep = jnp.where(strict, keep, 1.0)
            beta = jnp.where(strict, beta, 0.0)
        keep3 = keep.reshape(SB_RUN, SUBLANES, SB_BLOCK)
        beta3 = beta.reshape(SB_RUN, SUBLANES, SB_BLOCK)
        after = [None] * SB_RUN
        acc = jnp.ones((SUBLANES, SB_BLOCK), F32)
        for v in range(SB_RUN - 1, -1, -1):
            after[v] = acc
            acc = acc * keep3[v]
        later = run
        offs = [None] * SUBLANES
        for r in range(SUBLANES - 1, -1, -1):
            offs[r] = later
            later = later * acc[r:r + 1]
        base = jnp.concatenate(offs, axis=0)
        a = jnp.concatenate([beta3[v] * after[v] * base for v in range(SB_RUN)], axis=0)
        pv = jnp.dot(vt_ref[j, pair_lanes(head), :], a.astype(BF16),
                     preferred_element_type=F32)
        return pv, later

    def pair_rows(pv_first, pv_second):
        return jnp.where(first_head, pv_first, pv_second)

    has_prev = i >= 1
    prev = jnp.maximum(i - 1, 0)
    z_diag = [scores(i, head) for head in range(SB_HEADS)]
    z_prev = [scores(prev, head) for head in range(SB_HEADS)]
    runs, pvs = [], []
    for head in range(SB_HEADS):
        pv_d, run_d = block(i, head, z_diag[head], jnp.ones((1, SB_BLOCK), F32), True)
        pv_p, run_p = block(prev, head, z_prev[head], run_d, False)
        pvs.append(pv_d + jnp.where(has_prev, pv_p, 0.0))
        runs.append(run_p)
    accs = [pair_rows(pvs[2 * p], pvs[2 * p + 1]) for p in range(n_pairs)]

    def alive(rs):
        top = rs[0]
        for r in rs[1:]:
            top = jnp.maximum(top, r)
        return (jnp.max(top) >= SB_MIN_WEIGHT).astype(jnp.int32)

    def cond(c):
        return (c[0] >= 0) & (c[1] > 0)

    def body(c):
        j, old_runs, old_accs = c[0], c[2:2 + SB_HEADS], c[2 + SB_HEADS:]
        new = [block(j, head, scores(j, head), old_runs[head], False)
               for head in range(SB_HEADS)]
        new_runs = [run for _, run in new]
        new_accs = [old_accs[p] + pair_rows(new[2 * p][0], new[2 * p + 1][0])
                    for p in range(n_pairs)]
        return (j - 1, alive(new_runs), *new_runs, *new_accs)

    out = lax.while_loop(cond, body, (i - 2, alive(runs), *runs, *accs))
    for p in range(n_pairs):
        o_ref[:, p * LANES:(p + 1) * LANES] = out[2 + SB_HEADS + p].T.astype(o_ref.dtype)


def _sb_attention(proj, proj_ts, batch, seq):
    nq = seq // SB_BLOCK
    q_rb, k_cb, v_rb = (SBQ_RB * LANES // SB_WIDTH, SBK_CB * LANES // SB_WIDTH,
                        SBV_RB * LANES // SB_WIDTH)
    return pl.pallas_call(
        _sb_kernel,
        grid=(batch, nq),
        in_specs=[
            pl.BlockSpec((1, SB_WIDTH, SB_BLOCK), lambda b, i: (b * nq + i, q_rb, 0)),
            pl.BlockSpec((seq, SB_WIDTH), lambda b, i: (b, k_cb)),
            pl.BlockSpec((nq, SB_WIDTH, SB_BLOCK), lambda b, i: (b, v_rb, 0)),
        ],
        out_specs=pl.BlockSpec((SB_BLOCK, SB_WIDTH), lambda b, i: (b * nq + i, 0)),
        out_shape=jax.ShapeDtypeStruct((batch * seq, SB_WIDTH), BF16),
        compiler_params=pltpu.CompilerParams(
            dimension_semantics=("parallel", "arbitrary"),
            vmem_limit_bytes=VMEM_LIMIT),
        name="stickbreak_attention",
    )(proj_ts, proj, proj_ts)


def _diff_kernel(lam_ref, gs_ref, qt_ref, k_ref, vt_ref, o_ref,
                 acc_ref, m_ref, l_ref, sa_ref, sb_ref, *, lambda_init):
    i = pl.program_id(1)
    feat = lax.broadcasted_iota(jnp.int32, (LANES, DIFF_BLOCK), 0)
    key = lax.broadcasted_iota(jnp.int32, (DIFF_BLOCK, DIFF_BLOCK), 0)
    qry = lax.broadcasted_iota(jnp.int32, (DIFF_BLOCK, DIFF_BLOCK), 1)
    causal = key <= qry
    first_comp = feat < DIFF_QK_DIM

    def head_queries(h):
        qt = qt_ref[0, h * LANES:(h + 1) * LANES, :]
        zero = jnp.zeros_like(qt)
        return jnp.where(first_comp, qt, zero), jnp.where(first_comp, zero, qt)

    qts = [q for h in range(DIFF_HEADS) for q in head_queries(h)]

    lv = lam_ref[...]
    lam = (jnp.exp(jnp.sum(lv[0:1] * lv[1:2], axis=-1, keepdims=True))
           - jnp.exp(jnp.sum(lv[2:3] * lv[3:4], axis=-1, keepdims=True)) + lambda_init)

    def head_lanes(h):
        return slice(h * LANES, (h + 1) * LANES)

    def scores(j, s_ref):
        start = pl.multiple_of(j * DIFF_BLOCK, DIFF_BLOCK)
        for h in range(DIFF_HEADS):
            kb = k_ref[pl.ds(start, DIFF_BLOCK), head_lanes(h)]
            for c in range(2):
                s_ref[2 * h + c] = jnp.dot(kb, qts[2 * h + c], preferred_element_type=F32)

    def absorb(j, s_ref, masked):
        for h in range(DIFF_HEADS):
            vt = vt_ref[j, head_lanes(h), :]
            for c in range(2):
                n = 2 * h + c
                m = m_ref[n]
                s = s_ref[n]
                if masked:
                    s = jnp.where(causal, s, -jnp.inf)
                m_new = jnp.maximum(m, jnp.max(s, axis=0, keepdims=True))
                alpha = jnp.exp2(m - m_new)
                p = jnp.exp2(s - m_new)
                m_ref[n] = m_new
                l_ref[n] = alpha * l_ref[n] + jnp.sum(p, axis=0, keepdims=True)
                acc_ref[n] = alpha * acc_ref[n] + jnp.dot(vt, p.astype(BF16),
                                                          preferred_element_type=F32)

    acc_ref[...] = jnp.zeros_like(acc_ref)
    l_ref[...] = jnp.zeros_like(l_ref)
    m_ref[...] = jnp.full(m_ref.shape, -jnp.inf, F32)

    scores(0, sa_ref)

    def pair(t, _):
        b = 2 * t
        scores(b + 1, sb_ref)
        absorb(b, sa_ref, False)
        scores(b + 2, sa_ref)
        absorb(b + 1, sb_ref, False)
        return 0

    lax.fori_loop(0, i // 2, pair, 0)

    @pl.when(i % 2 == 0)
    def _():
        absorb(i, sa_ref, True)

    @pl.when(i % 2 == 1)
    def _():
        scores(i, sb_ref)
        absorb(i - 1, sa_ref, False)
        absorb(i, sb_ref, True)

    for h in range(DIFF_HEADS):
        ot = (acc_ref[2 * h] / l_ref[2 * h]
              - lam * (acc_ref[2 * h + 1] / l_ref[2 * h + 1]))
        ms = jnp.mean(ot * ot, axis=0, keepdims=True)
        ot = ot * lax.rsqrt(ms + EPS) * gs_ref[...] * (1.0 - lambda_init)
        o_ref[:, head_lanes(h)] = ot.T.astype(o_ref.dtype)


def _diff_attention(proj, proj_t, lam_rows, g_subln_col, batch, seq, lambda_init):
    nq = seq // DIFF_BLOCK
    n_chains = 2 * DIFF_HEADS
    q_rb, k_cb, v_rb = (DQ_RB * LANES // DIFF_V_WIDTH, DK_CB * LANES // DIFF_V_WIDTH,
                        DV_RB * LANES // DIFF_V_WIDTH)
    return pl.pallas_call(
        functools.partial(_diff_kernel, lambda_init=lambda_init),
        grid=(batch, nq),
        in_specs=[
            pl.BlockSpec((4, DIFF_QK_DIM), lambda b, i: (0, 0)),
            pl.BlockSpec((DIFF_V_DIM, 1), lambda b, i: (0, 0)),
            pl.BlockSpec((1, DIFF_QK_WIDTH, DIFF_BLOCK), lambda b, i: (b * nq + i, q_rb, 0)),
            pl.BlockSpec((seq, DIFF_QK_WIDTH), lambda b, i: (b, k_cb)),
            pl.BlockSpec((nq, DIFF_V_WIDTH, DIFF_BLOCK), lambda b, i: (b, v_rb, 0)),
        ],
        out_specs=pl.BlockSpec((DIFF_BLOCK, DIFF_V_WIDTH), lambda b, i: (b * nq + i, 0)),
        out_shape=jax.ShapeDtypeStruct((batch * seq, DIFF_V_WIDTH), BF16),
        scratch_shapes=[pltpu.VMEM((n_chains, DIFF_V_DIM, DIFF_BLOCK), F32),
                        pltpu.VMEM((n_chains, 1, DIFF_BLOCK), F32),
                        pltpu.VMEM((n_chains, 1, DIFF_BLOCK), F32),
                        pltpu.VMEM((n_chains, DIFF_BLOCK, DIFF_BLOCK), F32),
                        pltpu.VMEM((n_chains, DIFF_BLOCK, DIFF_BLOCK), F32)],
        compiler_params=pltpu.CompilerParams(
            dimension_semantics=("parallel", "arbitrary"),
            vmem_limit_bytes=VMEM_LIMIT),
        name="differential_attention",
    )(lam_rows, g_subln_col, proj_t, proj, proj_t)


def _post_kernel(oa_ref, ob_ref, ga_ref, gb_ref, x_ref, wua_ref, wub_ref, wo_ref, gn_ref,
                 wr_hi_ref, wr_lo_ref, br_ref,
                 x1_ref, h2_ref, slab_ref, cnt_ref, carry_ref):
    step_id = pl.program_id(0)

    @pl.when(step_id == 0)
    def _():
        carry_ref[...] = jnp.zeros_like(carry_ref)

    u_a = jnp.dot(oa_ref[...], wua_ref[...], preferred_element_type=F32)
    u_b = jnp.dot(ob_ref[...], wub_ref[...], preferred_element_type=F32)
    y = (jax.nn.sigmoid(ga_ref[...].astype(F32)) * u_a
         + jax.nn.sigmoid(gb_ref[...].astype(F32)) * u_b)
    x1 = x_ref[...] + jnp.dot(y.astype(BF16), wo_ref[...], preferred_element_type=F32)
    x1_ref[...] = x1
    ms = jnp.mean(x1 * x1, axis=-1, keepdims=True)
    h2 = x1 * lax.rsqrt(ms + EPS) * gn_ref[...]
    _store_row_tiles(h2_ref, h2)

    h_hi = h2.astype(BF16)
    h_lo = (h2 - h_hi.astype(F32)).astype(BF16)
    logits = (jnp.dot(h_hi, wr_hi_ref[...], preferred_element_type=F32)
              + jnp.dot(h_lo, wr_hi_ref[...], preferred_element_type=F32)
              + jnp.dot(h_hi, wr_lo_ref[...], preferred_element_type=F32)
              + br_ref[...])

    lane = lax.broadcasted_iota(jnp.int32, (POST_ROWS, LANES), 1)
    neg = -jnp.inf

    def first_argmax(v):
        m = jnp.max(v, axis=-1, keepdims=True)
        idx = jnp.min(jnp.where(v == m, lane, LANES), axis=-1, keepdims=True)
        return m, idx

    g_logits = jnp.where(lane < N_GROUPS, logits, neg)
    g_max, grp = first_argmax(g_logits)
    p_grp = 1.0 / jnp.sum(jnp.exp(g_logits - g_max), axis=-1, keepdims=True)

    lo_lane = ROUTER_EXPERT_LANE0 + grp * EXPERTS_PER_GROUP
    in_grp = (lane >= lo_lane) & (lane < lo_lane + EXPERTS_PER_GROUP)
    e_logits = jnp.where(in_grp, logits, neg)
    m1, i1 = first_argmax(e_logits)
    m2, i2 = first_argmax(jnp.where(lane == i1, neg, e_logits))
    r = jnp.exp(m2 - m1)
    w0 = p_grp / (1.0 + r)
    w1 = p_grp * r / (1.0 + r)

    hit0 = lane == i1
    hit1 = lane == i2
    onehot = jnp.where(hit0 | hit1, 1.0, 0.0)
    trow = lax.broadcasted_iota(jnp.int32, (POST_ROWS, POST_ROWS), 0)
    tcol = lax.broadcasted_iota(jnp.int32, (POST_ROWS, POST_ROWS), 1)
    earlier = jnp.where(tcol < trow, 1.0, 0.0).astype(BF16)
    before = carry_ref[0:1, :] + jnp.dot(earlier, onehot.astype(BF16),
                                         preferred_element_type=F32)
    r0 = jnp.sum(jnp.where(hit0, before, 0.0), axis=-1, keepdims=True)
    r1 = jnp.sum(jnp.where(hit1, before, 0.0), axis=-1, keepdims=True)
    new_carry = carry_ref[0:1, :] + jnp.sum(onehot, axis=0, keepdims=True)
    carry_ref[0:1, :] = new_carry
    cnt_ref[...] = jnp.broadcast_to(new_carry, cnt_ref.shape)

    e0 = (i1 - ROUTER_EXPERT_LANE0).astype(F32)
    e1 = (i2 - ROUTER_EXPERT_LANE0).astype(F32)
    slab = jnp.zeros((POST_ROWS, LANES), F32)
    for pos, val in ((SLAB_E0, e0), (SLAB_E1, e1), (SLAB_R0, r0), (SLAB_R1, r1),
                     (SLAB_W0, w0), (SLAB_W1, w1)):
        slab = jnp.where(lane == pos, val, slab)
    slab_ref[...] = slab


def _post_attention(o_a, o_b, proj, x2, wua, wub, wo, gn, wr_hi, wr_lo, br):
    n = x2.shape[0]
    const = lambda i: (0, 0)
    return pl.pallas_call(
        _post_kernel,
        grid=(n // POST_ROWS,),
        in_specs=[
            pl.BlockSpec((POST_ROWS, SB_WIDTH), lambda i: (i, 0)),
            pl.BlockSpec((POST_ROWS, DIFF_V_WIDTH), lambda i: (i, 0)),
            pl.BlockSpec((POST_ROWS, D_MODEL), lambda i: (i, GATE_A_B1024)),
            pl.BlockSpec((POST_ROWS, D_MODEL), lambda i: (i, GATE_B_B1024)),
            pl.BlockSpec((POST_ROWS, D_MODEL), lambda i: (i, 0)),
            pl.BlockSpec((SB_WIDTH, D_MODEL), const),
            pl.BlockSpec((DIFF_V_WIDTH, D_MODEL), const),
            pl.BlockSpec((D_MODEL, D_MODEL), const),
            pl.BlockSpec((1, D_MODEL), const),
            pl.BlockSpec((D_MODEL, LANES), const),
            pl.BlockSpec((D_MODEL, LANES), const),
            pl.BlockSpec((1, LANES), const),
        ],
        out_specs=[
            pl.BlockSpec((POST_ROWS, D_MODEL), lambda i: (i, 0)),
            pl.BlockSpec((POST_ROWS * ROW_TILE, LANES), lambda i: (i, 0)),
            pl.BlockSpec((POST_ROWS, LANES), lambda i: (i, 0)),
            pl.BlockSpec((8, LANES), const),
        ],
        out_shape=[
            jax.ShapeDtypeStruct((n, D_MODEL), F32),
            jax.ShapeDtypeStruct((n * ROW_TILE, LANES), F32),
            jax.ShapeDtypeStruct((n, LANES), F32),
            jax.ShapeDtypeStruct((8, LANES), F32),
        ],
        scratch_shapes=[pltpu.VMEM((8, LANES), F32)],
        compiler_params=pltpu.CompilerParams(
            dimension_semantics=("arbitrary",), vmem_limit_bytes=VMEM_LIMIT),
        name="merge_route",
    )(o_a, o_b, proj, proj, x2, wua, wub, wo, gn, wr_hi, wr_lo, br)


def _issue_row_copies(make_copy):
    def group(g, _):
        for u in range(ROW_DMA_UNROLL):
            for k in range(TOP_K):
                make_copy(g * ROW_DMA_UNROLL + u, k).start(priority=k % 2)
        return 0

    lax.fori_loop(0, ROW_DMA_TOKENS // ROW_DMA_UNROLL, group, 0)


def _scatter_kernel(dest_ref, pad_ref, h_ref, xd_ref, zero_tile, sem):
    n_pad = pad_ref.shape[-1]
    zero_tile[...] = jnp.zeros_like(zero_tile)
    _issue_row_copies(
        lambda t, k: _row_tile_copy(h_ref, t, xd_ref, dest_ref[0, 0, t * TOP_K + k], sem))
    for z in range(n_pad):
        _row_tile_copy(zero_tile, 0, xd_ref, pad_ref[0, 0, z], sem).start(priority=z % 2)
    for _ in range(TOP_K):
        pltpu.make_async_copy(
            h_ref, xd_ref.at[pl.ds(0, ROW_DMA_TOKENS * ROW_TILE), :], sem).wait()
    pltpu.make_async_copy(
        h_ref.at[pl.ds(0, n_pad * ROW_TILE), :], xd_ref.at[pl.ds(0, n_pad * ROW_TILE), :],
        sem).wait()


def _scatter_rows(dest3, pad3, h2_tiles, n_rows):
    n = h2_tiles.shape[0] // ROW_TILE
    steps = n // ROW_DMA_TOKENS
    assert pad3.shape[0] == steps and pad3.shape[2] <= ROW_DMA_TOKENS
    return pl.pallas_call(
        _scatter_kernel,
        grid=(steps,),
        in_specs=[
            pl.BlockSpec((1, 1, TOP_K * ROW_DMA_TOKENS), lambda i: (i, 0, 0),
                         memory_space=pltpu.SMEM),
            pl.BlockSpec((1, 1, pad3.shape[2]), lambda i: (i, 0, 0), memory_space=pltpu.SMEM),
            pl.BlockSpec((ROW_DMA_TOKENS * ROW_TILE, LANES), lambda i: (i, 0)),
        ],
        out_specs=pl.BlockSpec(memory_space=pl.ANY),
        out_shape=jax.ShapeDtypeStruct((n_rows * ROW_TILE, LANES), F32),
        scratch_shapes=[pltpu.VMEM((ROW_TILE, LANES), F32), pltpu.SemaphoreType.DMA],
        compiler_params=pltpu.CompilerParams(
            dimension_semantics=("arbitrary",), vmem_limit_bytes=VMEM_LIMIT),
        name="scatter_rows",
    )(dest3, pad3, h2_tiles)


def _expert_kernel(be_ref, nb_ref, x_ref, *refs):
    w_refs, y_ref = refs[:-1], refs[-1]
    first_block = pl.program_id(0) * MOE_GROUP

    @pl.when(first_block < nb_ref[0])
    def _():
        for blk in range(MOE_GROUP):
            wg_ref, wu_ref, wd_ref = w_refs[3 * blk:3 * blk + 3]
            x = _load_row_tiles(x_ref, MOE_BLOCK, first_row=blk * MOE_BLOCK).astype(BF16)
            g = jnp.dot(x, wg_ref[0], preferred_element_type=F32)
            u = jnp.dot(x, wu_ref[0], preferred_element_type=F32)
            hmid = (g * jax.nn.sigmoid(g) * u).astype(BF16)
            _store_row_tiles(y_ref, jnp.dot(hmid, wd_ref[0], preferred_element_type=F32),
                             first_row=blk * MOE_BLOCK)

    @pl.when(first_block >= nb_ref[0])
    def _():
        y_ref[...] = jnp.zeros_like(y_ref)


def _expert_mlp(block_e, n_used, xd_tiles, wg, wu, wd):
    n_blocks = xd_tiles.shape[0] // (MOE_BLOCK * ROW_TILE)
    assert n_blocks % MOE_GROUP == 0
    step_rows = MOE_GROUP * MOE_BLOCK * ROW_TILE

    def w_spec(shape, blk):
        return pl.BlockSpec((1,) + shape, lambda i, be, nb: (be[i * MOE_GROUP + blk], 0, 0))

    w_specs, w_args = [], []
    for blk in range(MOE_GROUP):
        w_specs += [w_spec((D_MODEL, D_EXPERT), blk), w_spec((D_MODEL, D_EXPERT), blk),
                    w_spec((D_EXPERT, D_MODEL), blk)]
        w_args += [wg, wu, wd]
    return pl.pallas_call(
        _expert_kernel,
        grid_spec=pltpu.PrefetchScalarGridSpec(
            num_scalar_prefetch=2,
            grid=(n_blocks // MOE_GROUP,),
            in_specs=[pl.BlockSpec((step_rows, LANES), lambda i, be, nb: (i, 0))] + w_specs,
            out_specs=pl.BlockSpec((step_rows, LANES), lambda i, be, nb: (i, 0)),
        ),
        out_shape=jax.ShapeDtypeStruct(xd_tiles.shape, F32),
        compiler_params=pltpu.CompilerParams(
            dimension_semantics=("arbitrary",), vmem_limit_bytes=VMEM_LIMIT),
        name="expert_mlp",
    )(block_e, n_used, xd_tiles, *w_args)


def _combine_kernel(dest_ref, next_dest_ref, x1_ref, slab_ref, g_ref, yd_ref, o_ref, ybuf, sems):
    step = pl.program_id(0)
    slot_rows = TOP_K * ROW_DMA_TOKENS

    def gather(idx_ref, slot):
        _issue_row_copies(
            lambda t, k: _row_tile_copy(yd_ref, idx_ref[0, 0, t * TOP_K + k], ybuf,
                                        slot * slot_rows + k * ROW_DMA_TOKENS + t,
                                        sems.at[slot]))

    @pl.when(step == 0)
    def _():
        gather(dest_ref, 0)

    @pl.when(step + 1 < pl.num_programs(0))
    def _():
        gather(next_dest_ref, (step + 1) % 2)

    slot = step % 2
    first = slot * slot_rows
    pltpu.make_async_copy(
        yd_ref.at[pl.ds(0, slot_rows * ROW_TILE), :],
        ybuf.at[pl.ds(pl.multiple_of(first * ROW_TILE, ROW_TILE), slot_rows * ROW_TILE), :],
        sems.at[slot]).wait()

    slab = slab_ref[...]
    w0 = slab[:, SLAB_W0:SLAB_W0 + 1]
    w1 = slab[:, SLAB_W1:SLAB_W1 + 1]
    x = (x1_ref[...] + w0 * _load_row_tiles(ybuf, ROW_DMA_TOKENS, first_row=first)
         + w1 * _load_row_tiles(ybuf, ROW_DMA_TOKENS, first_row=first + ROW_DMA_TOKENS))
    ms = jnp.mean(x * x, axis=-1, keepdims=True)
    o_ref[...] = x * lax.rsqrt(ms + EPS) * g_ref[...]


def _combine(dest3, x1, slab, g_final, yd_tiles):
    n = x1.shape[0]
    steps = n // ROW_DMA_TOKENS
    return pl.pallas_call(
        _combine_kernel,
        grid=(steps,),
        in_specs=[
            pl.BlockSpec((1, 1, TOP_K * ROW_DMA_TOKENS), lambda i: (i, 0, 0),
                         memory_space=pltpu.SMEM),
            pl.BlockSpec((1, 1, TOP_K * ROW_DMA_TOKENS),
                         lambda i: (jnp.minimum(i + 1, steps - 1), 0, 0),
                         memory_space=pltpu.SMEM),
            pl.BlockSpec((ROW_DMA_TOKENS, D_MODEL), lambda i: (i, 0)),
            pl.BlockSpec((ROW_DMA_TOKENS, LANES), lambda i: (i, 0)),
            pl.BlockSpec((1, D_MODEL), lambda i: (0, 0)),
            pl.BlockSpec(memory_space=pl.ANY),
        ],
        out_specs=pl.BlockSpec((ROW_DMA_TOKENS, D_MODEL), lambda i: (i, 0)),
        out_shape=jax.ShapeDtypeStruct((n, D_MODEL), F32),
        scratch_shapes=[pltpu.VMEM((2 * TOP_K * ROW_DMA_TOKENS * ROW_TILE, LANES), F32),
                        pltpu.SemaphoreType.DMA((2,))],
        compiler_params=pltpu.CompilerParams(
            dimension_semantics=("arbitrary",), vmem_limit_bytes=VMEM_LIMIT),
        name="combine_norm",
    )(dest3, dest3, x1, slab, g_final, yd_tiles)


def _rope_tables(seq):
    pos = jnp.arange(seq, dtype=F32)
    inv_freq = 1.0 / (ROPE_THETA ** (jnp.arange(0, DIFF_QK_DIM, 2, dtype=F32) / DIFF_QK_DIM))
    freqs = pos[:, None] * inv_freq[None, :]
    emb = jnp.concatenate([freqs, freqs], axis=-1)
    cos, sin = jnp.cos(emb), jnp.sin(emb)
    half = DIFF_QK_DIM // 2
    sin_signed = jnp.concatenate([-sin[:, :half], sin[:, half:]], axis=-1)
    reps = LANES // DIFF_QK_DIM
    return (jnp.tile(cos, (1, reps)), jnp.tile(sin_signed, (1, reps)), cos.T, sin_signed.T)


def _split_in_projection(w):
    edges = [0]
    for width in (SB_WIDTH, SB_WIDTH, SB_WIDTH, DIFF_QK_WIDTH, DIFF_QK_WIDTH, DIFF_V_WIDTH,
                  D_MODEL, D_MODEL):
        edges.append(edges[-1] + width)
    sb_q, sb_k, sb_v, d_q, d_k, d_v, gate_a, gate_b = (
        w[:, lo:hi] for lo, hi in zip(edges[:-1], edges[1:]))
    w_rows = jnp.concatenate([sb_k, d_k, gate_a, gate_b], axis=1)
    w_feat = jnp.concatenate([d_q, d_v], axis=1)
    w_feat_sb = jnp.concatenate([sb_q, sb_v], axis=1)
    return w_rows.astype(BF16), w_feat.astype(BF16), w_feat_sb.astype(BF16)


def _split_bf16(w):
    hi = w.astype(BF16)
    lo = (w - hi.astype(F32)).astype(BF16)
    return hi, lo


def kernel(x, g_norm_mix, w_in, lambda_q1, lambda_k1, lambda_q2, lambda_k2, g_subln,
           w_up_a, w_up_b, w_out, g_norm_ffn, w_router_group, b_router_group,
           w_router_expert, b_router_expert, w_expert_gate, w_expert_up, w_expert_down,
           g_norm_final):
    batch, seq, d = x.shape
    depth = w_in.shape[0]
    n_tok = batch * seq
    assert depth == 1, "the final RMSNorm is fused into the layer's combine step"
    assert d == D_MODEL and seq % DIFF_BLOCK == 0 and seq % PROJ_ROWS == 0
    assert seq % SB_BLOCK == 0 and PROJ_ROWS % SB_BLOCK == 0
    assert DIFF_BLOCK == PROJ_ROWS, "feature-major q/v blocks are written per projection step"
    assert (N_EXPERTS * MOE_BLOCK) % (n_tok // ROW_DMA_TOKENS) == 0
    assert n_tok % POST_ROWS == 0 and n_tok % ROW_DMA_TOKENS == 0
    n_assign = n_tok * TOP_K
    n_blocks = n_assign // MOE_BLOCK + N_EXPERTS
    n_rows = n_blocks * MOE_BLOCK

    rope = _rope_tables(seq)
    x2 = x.reshape(n_tok, d)
    for l in range(depth):
        lambda_init = 0.8 - 0.6 * math.exp(-0.3 * l)
        w_rows, w_feat, w_feat_sb = _split_in_projection(w_in[l])
        proj, proj_t, proj_ts = _in_projection(x2, g_norm_mix[l][None, :], w_rows, w_feat,
                                               w_feat_sb, rope, seq)
        o_a = _sb_attention(proj, proj_ts, batch, seq)
        lam_rows = jnp.stack([lambda_q1[l], lambda_k1[l], lambda_q2[l], lambda_k2[l]])
        o_b = _diff_attention(proj, proj_t, lam_rows, g_subln[l][:, None], batch, seq,
                              lambda_init)

        w_r = jnp.zeros((d, LANES), F32)
        w_r = w_r.at[:, :N_GROUPS].set(w_router_group[l])
        w_r = w_r.at[:, N_GROUPS:N_GROUPS + N_EXPERTS].set(w_router_expert[l])
        b_r = jnp.zeros((1, LANES), F32)
        b_r = b_r.at[0, :N_GROUPS].set(b_router_group[l])
        b_r = b_r.at[0, N_GROUPS:N_GROUPS + N_EXPERTS].set(b_router_expert[l])
        wr_hi, wr_lo = _split_bf16(w_r)

        x1, h2, slab, cnt = _post_attention(
            o_a, o_b, proj, x2, w_up_a[l].astype(BF16), w_up_b[l].astype(BF16),
            w_out[l].astype(BF16), g_norm_ffn[l][None, :], wr_hi, wr_lo, b_r)

        counts = cnt[0, N_GROUPS:N_GROUPS + N_EXPERTS].astype(jnp.int32)
        padded = (counts + MOE_BLOCK - 1) // MOE_BLOCK * MOE_BLOCK
        pad_end = jnp.cumsum(padded)
        pad_start = pad_end - padded
        e_ids = slab[:, SLAB_E0:SLAB_E1 + 1].astype(jnp.int32)
        ranks = slab[:, SLAB_R0:SLAB_R1 + 1].astype(jnp.int32)
        expert_iota = jnp.arange(N_EXPERTS, dtype=jnp.int32)
        dest = ranks + jnp.sum(
            jnp.where(e_ids[:, :, None] == expert_iota, pad_start, 0), axis=-1)
        steps = n_tok // ROW_DMA_TOKENS
        dest3 = dest.reshape(steps, 1, TOP_K * ROW_DMA_TOKENS)
        gap = padded - counts
        gap_end = jnp.cumsum(gap)
        z = jnp.arange(n_rows - n_assign, dtype=jnp.int32)
        seg = jnp.sum((gap_end[None, :] <= z[:, None]).astype(jnp.int32), axis=1)
        seg_onehot = seg[:, None] == jnp.arange(N_EXPERTS + 1, dtype=jnp.int32)
        seg_first_row = jnp.concatenate([pad_start + counts, pad_end[-1:]])
        seg_first_z = jnp.concatenate([gap_end - gap, gap_end[-1:]])
        pad_rows = z + jnp.sum(jnp.where(seg_onehot, seg_first_row - seg_first_z, 0), axis=1)
        pad3 = pad_rows.reshape(steps, 1, (n_rows - n_assign) // steps)
        block_start = jnp.arange(n_blocks, dtype=jnp.int32) * MOE_BLOCK
        block_e = jnp.minimum(
            jnp.sum((pad_end[None, :] <= block_start[:, None]).astype(jnp.int32), axis=1),
            N_EXPERTS - 1)
        n_used = (pad_end[-1:] // MOE_BLOCK).astype(jnp.int32)

        xd = _scatter_rows(dest3, pad3, h2, n_rows)
        yd = _expert_mlp(block_e, n_used, xd, w_expert_gate[l].astype(BF16),
                         w_expert_up[l].astype(BF16), w_expert_down[l].astype(BF16))
        x2 = _combine(dest3, x1, slab, g_norm_final[None, :], yd)
    return x2.reshape(batch, seq, d)
```

```python
import functools
import math

import jax
import jax.numpy as jnp
from jax import lax
from jax.experimental import pallas as pl
from jax.experimental.pallas import tpu as pltpu

D_MODEL = 1024
HEAD_DIM = 64
SB_HEADS = 8
SB_WIDTH = SB_HEADS * HEAD_DIM
DIFF_HEADS = 4
DIFF_QK_DIM = 64
DIFF_V_DIM = 2 * DIFF_QK_DIM
DIFF_QK_WIDTH = DIFF_HEADS * 2 * DIFF_QK_DIM
DIFF_V_WIDTH = DIFF_HEADS * DIFF_V_DIM
IN_WIDTH = 3 * SB_WIDTH + 2 * DIFF_QK_WIDTH + DIFF_V_WIDTH + 2 * D_MODEL
ROPE_THETA = 10000.0
N_GROUPS = 4
EXPERTS_PER_GROUP = 8
N_EXPERTS = N_GROUPS * EXPERTS_PER_GROUP
TOP_K = 2
D_EXPERT = 256
EPS = 1e-6

LANES = 128
SUBLANES = 8
QK_SCALE = HEAD_DIM ** -0.5
LOG2E = math.log2(math.e)

F32 = jnp.float32
BF16 = jnp.bfloat16

PROJ_ROWS = 512
PROJ_CHUNK = 512
SB_BLOCK = 256
DIFF_BLOCK = 512
SB_MIN_WEIGHT = 2.0 ** -126
SB_MAX_LOG2 = 126.0
POST_ROWS = 512
MOE_BLOCK = 256
MOE_GROUP = 4
ROW_DMA_TOKENS = 512
ROW_DMA_UNROLL = 8
VMEM_LIMIT = 56 * 1024 * 1024

SB_RUN = SB_BLOCK // SUBLANES
ROW_WIDTH = SB_WIDTH + DIFF_QK_WIDTH + 2 * D_MODEL
T_WIDTH = DIFF_QK_WIDTH + DIFF_V_WIDTH
TS_WIDTH = 2 * SB_WIDTH
SBK_CB, DK_CB = 0, 4
GATE_A_B1024, GATE_B_B1024 = 1, 2
DQ_RB, DV_RB = 0, DIFF_QK_WIDTH // LANES
SBQ_RB, SBV_RB = 0, SB_WIDTH // LANES

SLAB_E0, SLAB_E1, SLAB_R0, SLAB_R1, SLAB_W0, SLAB_W1 = 0, 1, 2, 3, 4, 5
ROUTER_EXPERT_LANE0 = N_GROUPS


def _nt_dot(a, b):
    return lax.dot_general(a, b, (((1,), (1,)), ((), ())), preferred_element_type=F32)


def _tnt_dot(a, b):
    return lax.dot_general(a, b, (((0,), (1,)), ((), ())), preferred_element_type=F32)


ROW_TILE = D_MODEL // LANES


def _store_row_tiles(ref, value, first_row=0):
    rows = value.shape[0]
    for k in range(ROW_TILE):
        ref[pl.ds(first_row * ROW_TILE + k, rows, stride=ROW_TILE), :] = (
            value[:, k * LANES:(k + 1) * LANES])


def _load_row_tiles(ref, rows, first_row=0):
    return jnp.concatenate(
        [ref[pl.ds(first_row * ROW_TILE + k, rows, stride=ROW_TILE), :]
         for k in range(ROW_TILE)], axis=1)


def _row_tile_copy(src_ref, src_row, dst_ref, dst_row, sem):
    src = src_ref.at[pl.ds(pl.multiple_of(src_row * ROW_TILE, ROW_TILE), ROW_TILE), :]
    dst = dst_ref.at[pl.ds(pl.multiple_of(dst_row * ROW_TILE, ROW_TILE), ROW_TILE), :]
    return pltpu.make_async_copy(src, dst, sem)


def _inproj_kernel(x_ref, g_ref, w_ref, wt_ref, wts_ref, cos_ref, sin_ref, cost_ref, sint_ref,
                   o_ref, ot_ref, ots_ref):
    def normed(x):
        ms = jnp.mean(x * x, axis=-1, keepdims=True)
        return (x * lax.rsqrt(ms + EPS) * g_ref[...]).astype(BF16)

    h = normed(x_ref[...])
    dst = lax.broadcasted_iota(jnp.int32, (SB_BLOCK, SB_BLOCK), 0)
    src = lax.broadcasted_iota(jnp.int32, (SB_BLOCK, SB_BLOCK), 1)
    perm = jnp.where(src == (dst % SUBLANES) * SB_RUN + dst // SUBLANES, 1.0, 0.0).astype(BF16)
    h_perm = jnp.concatenate(
        [jnp.dot(perm, h[base:base + SB_BLOCK], preferred_element_type=F32).astype(BF16)
         for base in range(0, PROJ_ROWS, SB_BLOCK)], axis=0)

    lane = lax.broadcasted_iota(jnp.int32, (PROJ_ROWS, LANES), 1)
    first_half = (lane % DIFF_QK_DIM) < (DIFF_QK_DIM // 2)
    cos = cos_ref[...]
    sin = sin_ref[...]

    def rope(acc):
        outs = []
        for s in range(PROJ_CHUNK // LANES):
            a = acc[:, s * LANES:(s + 1) * LANES]
            swapped = jnp.where(first_half,
                                pltpu.roll(a, LANES - DIFF_QK_DIM // 2, 1),
                                pltpu.roll(a, DIFF_QK_DIM // 2, 1))
            outs.append(a * cos + swapped * sin)
        return jnp.concatenate(outs, axis=1)

    for c in range(ROW_WIDTH // PROJ_CHUNK):
        lo = c * PROJ_CHUNK
        lhs = h_perm if lo == SBK_CB * LANES else h
        acc = jnp.dot(lhs, w_ref[:, lo:lo + PROJ_CHUNK], preferred_element_type=F32)
        if lo == DK_CB * LANES:
            acc = rope(acc)
        o_ref[:, lo:lo + PROJ_CHUNK] = acc.astype(BF16)

    qt = _tnt_dot(wt_ref[:, 0:DIFF_QK_WIDTH], h)
    cos_f = cost_ref[...]
    sin_f = sint_ref[...]
    half = DIFF_QK_DIM // 2
    groups = []
    for gidx in range(DIFF_QK_WIDTH // DIFF_QK_DIM):
        xg = qt[gidx * DIFF_QK_DIM:(gidx + 1) * DIFF_QK_DIM]
        swapped = jnp.concatenate([xg[half:], xg[:half]], axis=0)
        groups.append((xg * cos_f + swapped * sin_f) * (QK_SCALE * LOG2E))
    ot_ref[0, 0:DIFF_QK_WIDTH, :] = jnp.concatenate(groups, axis=0).astype(BF16)
    ot_ref[0, DIFF_QK_WIDTH:T_WIDTH, :] = _tnt_dot(wt_ref[:, DIFF_QK_WIDTH:T_WIDTH],
                                                   h).astype(BF16)

    sq = (_tnt_dot(wts_ref[:, 0:SB_WIDTH], h) * (QK_SCALE * LOG2E)).astype(BF16)
    sv = _tnt_dot(wts_ref[:, SB_WIDTH:TS_WIDTH], h_perm).astype(BF16)
    for blk in range(PROJ_ROWS // SB_BLOCK):
        cols = slice(blk * SB_BLOCK, (blk + 1) * SB_BLOCK)
        ots_ref[blk, 0:SB_WIDTH, :] = sq[:, cols]
        ots_ref[blk, SB_WIDTH:TS_WIDTH, :] = sv[:, cols]


def _in_projection(x2, g, w_rows, w_feat, w_feat_sb, rope, seq):
    n = x2.shape[0]
    pos_blocks = seq // PROJ_ROWS
    sb_per_step = PROJ_ROWS // SB_BLOCK
    cos_t, sin_t, cos_f, sin_f = rope
    const = lambda i: (0, 0)
    return pl.pallas_call(
        _inproj_kernel,
        grid=(n // PROJ_ROWS,),
        in_specs=[
            pl.BlockSpec((PROJ_ROWS, D_MODEL), lambda i: (i, 0)),
            pl.BlockSpec((1, D_MODEL), const),
            pl.BlockSpec((D_MODEL, ROW_WIDTH), const),
            pl.BlockSpec((D_MODEL, T_WIDTH), const),
            pl.BlockSpec((D_MODEL, TS_WIDTH), const),
            pl.BlockSpec((PROJ_ROWS, LANES), lambda i: (i % pos_blocks, 0)),
            pl.BlockSpec((PROJ_ROWS, LANES), lambda i: (i % pos_blocks, 0)),
            pl.BlockSpec((DIFF_QK_DIM, PROJ_ROWS), lambda i: (0, i % pos_blocks)),
            pl.BlockSpec((DIFF_QK_DIM, PROJ_ROWS), lambda i: (0, i % pos_blocks)),
        ],
        out_specs=[
            pl.BlockSpec((PROJ_ROWS, ROW_WIDTH), lambda i: (i, 0)),
            pl.BlockSpec((1, T_WIDTH, PROJ_ROWS), lambda i: (i, 0, 0)),
            pl.BlockSpec((sb_per_step, TS_WIDTH, SB_BLOCK), lambda i: (i, 0, 0)),
        ],
        out_shape=[
            jax.ShapeDtypeStruct((n, ROW_WIDTH), BF16),
            jax.ShapeDtypeStruct((n // PROJ_ROWS, T_WIDTH, PROJ_ROWS), BF16),
            jax.ShapeDtypeStruct((n // SB_BLOCK, TS_WIDTH, SB_BLOCK), BF16),
        ],
        compiler_params=pltpu.CompilerParams(
            dimension_semantics=("parallel",), vmem_limit_bytes=VMEM_LIMIT),
        name="in_projection",
    )(x2, g, w_rows, w_feat, w_feat_sb, cos_t, sin_t, cos_f, sin_f)


def _sb_kernel(qt_ref, k_ref, vt_ref, o_ref):
    i = pl.program_id(1)
    n_pairs = SB_WIDTH // LANES
    feat = lax.broadcasted_iota(jnp.int32, (LANES, SB_BLOCK), 0)
    pos = lax.broadcasted_iota(jnp.int32, (SB_BLOCK, SB_BLOCK), 0)
    qry = lax.broadcasted_iota(jnp.int32, (SB_BLOCK, SB_BLOCK), 1)
    key = (pos % SUBLANES) * SB_RUN + pos // SUBLANES
    strict = key < qry
    first_head = feat < HEAD_DIM

    def head_queries(p):
        qt = qt_ref[0, p * LANES:(p + 1) * LANES, :]
        zero = jnp.zeros_like(qt)
        return jnp.where(first_head, qt, zero), jnp.where(first_head, zero, qt)

    qts = [q for p in range(n_pairs) for q in head_queries(p)]

    def pair_lanes(head):
        return slice((head // 2) * LANES, (head // 2 + 1) * LANES)

    def scores(j, head):
        start = pl.multiple_of(j * SB_BLOCK, SB_BLOCK)
        return jnp.dot(k_ref[pl.ds(start, SB_BLOCK), pair_lanes(head)], qts[head],
                       preferred_element_type=F32)

    def block(j, head, z, run, masked):
        e = jnp.exp2(jnp.minimum(z, SB_MAX_LOG2))
        keep = 1.0 / (1.0 + e)
        beta = e * keep
        if masked:
            keep = jnp.where(strict, keep, 1.0)
            beta = jnp.where(strict, beta, 0.0)
        keep3 = keep.reshape(SB_RUN, SUBLANES, SB_BLOCK)
        beta3 = beta.reshape(SB_RUN, SUBLANES, SB_BLOCK)
        after = [None] * SB_RUN
        acc = jnp.ones((SUBLANES, SB_BLOCK), F32)
        for v in range(SB_RUN - 1, -1, -1):
            after[v] = acc
            acc = acc * keep3[v]
        later = run
        offs = [None] * SUBLANES
        for r in range(SUBLANES - 1, -1, -1):
            offs[r] = later
            later = later * acc[r:r + 1]
        base = jnp.concatenate(offs, axis=0)
        a = jnp.concatenate([beta3[v] * after[v] * base for v in range(SB_RUN)], axis=0)
        pv = jnp.dot(vt_ref[j, pair_lanes(head), :], a.astype(BF16),
                     preferred_element_type=F32)
        return pv, later

    def pair_rows(pv_first, pv_second):
        return jnp.where(first_head, pv_first, pv_second)

    has_prev = i >= 1
    prev = jnp.maximum(i - 1, 0)
    z_diag = [scores(i, head) for head in range(SB_HEADS)]
    z_prev = [scores(prev, head) for head in range(SB_HEADS)]
    runs, pvs = [], []
    for head in range(SB_HEADS):
        pv_d, run_d = block(i, head, z_diag[head], jnp.ones((1, SB_BLOCK), F32), True)
        pv_p, run_p = block(prev, head, z_prev[head], run_d, False)
        pvs.append(pv_d + jnp.where(has_prev, pv_p, 0.0))
        runs.append(run_p)
    accs = [pair_rows(pvs[2 * p], pvs[2 * p + 1]) for p in range(n_pairs)]

    def alive(rs):
        top = rs[0]
        for r in rs[1:]:
            top = jnp.maximum(top, r)
        return (jnp.max(top) >= SB_MIN_WEIGHT).astype(jnp.int32)

    def cond(c):
        return (c[0] >= 0) & (c[1] > 0)

    def body(c):
        j, old_runs, old_accs = c[0], c[2:2 + SB_HEADS], c[2 + SB_HEADS:]
        new = [block(j, head, scores(j, head), old_runs[head], False)
               for head in range(SB_HEADS)]
        new_runs = [run for _, run in new]
        new_accs = [old_accs[p] + pair_rows(new[2 * p][0], new[2 * p + 1][0])
                    for p in range(n_pairs)]
        return (j - 1, alive(new_runs), *new_runs, *new_accs)

    out = lax.while_loop(cond, body, (i - 2, alive(runs), *runs, *accs))
    for p in range(n_pairs):
        o_ref[:, p * LANES:(p + 1) * LANES] = out[2 + SB_HEADS + p].T.astype(o_ref.dtype)


def _sb_attention(proj, proj_ts, batch, seq):
    nq = seq // SB_BLOCK
    q_rb, k_cb, v_rb = (SBQ_RB * LANES // SB_WIDTH, SBK_CB * LANES // SB_WIDTH,
                        SBV_RB * LANES // SB_WIDTH)
    return pl.pallas_call(
        _sb_kernel,
        grid=(batch, nq),
        in_specs=[
            pl.BlockSpec((1, SB_WIDTH, SB_BLOCK), lambda b, i: (b * nq + i, q_rb, 0)),
            pl.BlockSpec((seq, SB_WIDTH), lambda b, i: (b, k_cb)),
            pl.BlockSpec((nq, SB_WIDTH, SB_BLOCK), lambda b, i: (b, v_rb, 0)),
        ],
        out_specs=pl.BlockSpec((SB_BLOCK, SB_WIDTH), lambda b, i: (b * nq + i, 0)),
        out_shape=jax.ShapeDtypeStruct((batch * seq, SB_WIDTH), BF16),
        compiler_params=pltpu.CompilerParams(
            dimension_semantics=("parallel", "arbitrary"),
            vmem_limit_bytes=VMEM_LIMIT),
        name="stickbreak_attention",
    )(proj_ts, proj, proj_ts)


def _diff_kernel(lam_ref, gs_ref, qt_ref, k_ref, vt_ref, o_ref,
                 acc_ref, m_ref, l_ref, sa_ref, sb_ref, *, lambda_init):
    i = pl.program_id(1)
    feat = lax.broadcasted_iota(jnp.int32, (LANES, DIFF_BLOCK), 0)
    key = lax.broadcasted_iota(jnp.int32, (DIFF_BLOCK, DIFF_BLOCK), 0)
    qry = lax.broadcasted_iota(jnp.int32, (DIFF_BLOCK, DIFF_BLOCK), 1)
    causal = key <= qry
    first_comp = feat < DIFF_QK_DIM

    def head_queries(h):
        qt = qt_ref[0, h * LANES:(h + 1) * LANES, :]
        zero = jnp.zeros_like(qt)
        return jnp.where(first_comp, qt, zero), jnp.where(first_comp, zero, qt)

    qts = [q for h in range(DIFF_HEADS) for q in head_queries(h)]

    lv = lam_ref[...]
    lam = (jnp.exp(jnp.sum(lv[0:1] * lv[1:2], axis=-1, keepdims=True))
           - jnp.exp(jnp.sum(lv[2:3] * lv[3:4], axis=-1, keepdims=True)) + lambda_init)

    def head_lanes(h):
        return slice(h * LANES, (h + 1) * LANES)

    def scores(j, s_ref):
        start = pl.multiple_of(j * DIFF_BLOCK, DIFF_BLOCK)
        for h in range(DIFF_HEADS):
            kb = k_ref[pl.ds(start, DIFF_BLOCK), head_lanes(h)]
            for c in range(2):
                s_ref[2 * h + c] = jnp.dot(kb, qts[2 * h + c], preferred_element_type=F32)

    def absorb(j, s_ref, masked):
        for h in range(DIFF_HEADS):
            vt = vt_ref[j, head_lanes(h), :]
            for c in range(2):
                n = 2 * h + c
                m = m_ref[n]
                s = s_ref[n]
                if masked:
                    s = jnp.where(causal, s, -jnp.inf)
                m_new = jnp.maximum(m, jnp.max(s, axis=0, keepdims=True))
                alpha = jnp.exp2(m - m_new)
                p = jnp.exp2(s - m_new)
                m_ref[n] = m_new
                l_ref[n] = alpha * l_ref[n] + jnp.sum(p, axis=0, keepdims=True)
                acc_ref[n] = alpha * acc_ref[n] + jnp.dot(vt, p.astype(BF16),
                                                          preferred_element_type=F32)

    acc_ref[...] = jnp.zeros_like(acc_ref)
    l_ref[...] = jnp.zeros_like(l_ref)
    m_ref[...] = jnp.full(m_ref.shape, -jnp.inf, F32)

    scores(0, sa_ref)

    def pair(t, _):
        b = 2 * t
        scores(b + 1, sb_ref)
        absorb(b, sa_ref, False)
        scores(b + 2, sa_ref)
        absorb(b + 1, sb_ref, False)
        return 0

    lax.fori_loop(0, i // 2, pair, 0)

    @pl.when(i % 2 == 0)
    def _():
        absorb(i, sa_ref, True)

    @pl.when(i % 2 == 1)
    def _():
        scores(i, sb_ref)
        absorb(i - 1, sa_ref, False)
        absorb(i, sb_ref, True)

    for h in range(DIFF_HEADS):
        ot = (acc_ref[2 * h] / l_ref[2 * h]
              - lam * (acc_ref[2 * h + 1] / l_ref[2 * h + 1]))
        ms = jnp.mean(ot * ot, axis=0, keepdims=True)
        ot = ot * lax.rsqrt(ms + EPS) * gs_ref[...] * (1.0 - lambda_init)
        o_ref[:, head_lanes(h)] = ot.T.astype(o_ref.dtype)


def _diff_attention(proj, proj_t, lam_rows, g_subln_col, batch, seq, lambda_init):
    nq = seq // DIFF_BLOCK
    n_chains = 2 * DIFF_HEADS
    q_rb, k_cb, v_rb = (DQ_RB * LANES // DIFF_V_WIDTH, DK_CB * LANES // DIFF_V_WIDTH,
                        DV_RB * LANES // DIFF_V_WIDTH)
    return pl.pallas_call(
        functools.partial(_diff_kernel, lambda_init=lambda_init),
        grid=(batch, nq),
        in_specs=[
            pl.BlockSpec((4, DIFF_QK_DIM), lambda b, i: (0, 0)),
            pl.BlockSpec((DIFF_V_DIM, 1), lambda b, i: (0, 0)),
            pl.BlockSpec((1, DIFF_QK_WIDTH, DIFF_BLOCK), lambda b, i: (b * nq + i, q_rb, 0)),
            pl.BlockSpec((seq, DIFF_QK_WIDTH), lambda b, i: (b, k_cb)),
            pl.BlockSpec((nq, DIFF_V_WIDTH, DIFF_BLOCK), lambda b, i: (b, v_rb, 0)),
        ],
        out_specs=pl.BlockSpec((DIFF_BLOCK, DIFF_V_WIDTH), lambda b, i: (b * nq + i, 0)),
        out_shape=jax.ShapeDtypeStruct((batch * seq, DIFF_V_WIDTH), BF16),
        scratch_shapes=[pltpu.VMEM((n_chains, DIFF_V_DIM, DIFF_BLOCK), F32),
                        pltpu.VMEM((n_chains, 1, DIFF_BLOCK), F32),
                        pltpu.VMEM((n_chains, 1, DIFF_BLOCK), F32),
                        pltpu.VMEM((n_chains, DIFF_BLOCK, DIFF_BLOCK), F32),
                        pltpu.VMEM((n_chains, DIFF_BLOCK, DIFF_BLOCK), F32)],
        compiler_params=pltpu.CompilerParams(
            dimension_semantics=("parallel", "arbitrary"),
            vmem_limit_bytes=VMEM_LIMIT),
        name="differential_attention",
    )(lam_rows, g_subln_col, proj_t, proj, proj_t)


def _post_kernel(oa_ref, ob_ref, ga_ref, gb_ref, x_ref, wua_ref, wub_ref, wo_ref, gn_ref,
                 wr_ref, br_ref,
                 x1_ref, h2_ref, slab_ref, cnt_ref, carry_ref):
    step_id = pl.program_id(0)

    @pl.when(step_id == 0)
    def _():
        carry_ref[...] = jnp.zeros_like(carry_ref)

    u_a = jnp.dot(oa_ref[...], wua_ref[...], preferred_element_type=F32)
    u_b = jnp.dot(ob_ref[...], wub_ref[...], preferred_element_type=F32)
    y = (jax.nn.sigmoid(ga_ref[...].astype(F32)) * u_a
         + jax.nn.sigmoid(gb_ref[...].astype(F32)) * u_b)
    x1 = x_ref[...] + jnp.dot(y.astype(BF16), wo_ref[...], preferred_element_type=F32)
    x1_ref[...] = x1
    ms = jnp.mean(x1 * x1, axis=-1, keepdims=True)
    h2 = x1 * lax.rsqrt(ms + EPS) * gn_ref[...]
    _store_row_tiles(h2_ref, h2)

    h_hi = h2.astype(BF16)
    h_lo = (h2 - h_hi.astype(F32)).astype(BF16)
    hi_terms = jnp.dot(h_hi, wr_ref[...], preferred_element_type=F32)
    logits = (hi_terms[:, :LANES] + hi_terms[:, LANES:]
              + jnp.dot(h_lo, wr_ref[:, :LANES], preferred_element_type=F32)
              + br_ref[...])

    lane = lax.broadcasted_iota(jnp.int32, (POST_ROWS, LANES), 1)
    neg = -jnp.inf

    def first_argmax(v):
        m = jnp.max(v, axis=-1, keepdims=True)
        idx = jnp.min(jnp.where(v == m, lane, LANES), axis=-1, keepdims=True)
        return m, idx

    g_logits = jnp.where(lane < N_GROUPS, logits, neg)
    g_max, grp = first_argmax(g_logits)
    p_grp = 1.0 / jnp.sum(jnp.exp(g_logits - g_max), axis=-1, keepdims=True)

    lo_lane = ROUTER_EXPERT_LANE0 + grp * EXPERTS_PER_GROUP
    in_grp = (lane >= lo_lane) & (lane < lo_lane + EXPERTS_PER_GROUP)
    e_logits = jnp.where(in_grp, logits, neg)
    m1, i1 = first_argmax(e_logits)
    m2, i2 = first_argmax(jnp.where(lane == i1, neg, e_logits))
    r = jnp.exp(m2 - m1)
    w0 = p_grp / (1.0 + r)
    w1 = p_grp * r / (1.0 + r)

    hit0 = lane == i1
    hit1 = lane == i2
    onehot = jnp.where(hit0 | hit1, 1.0, 0.0)
    trow = lax.broadcasted_iota(jnp.int32, (POST_ROWS, POST_ROWS), 0)
    tcol = lax.broadcasted_iota(jnp.int32, (POST_ROWS, POST_ROWS), 1)
    earlier = jnp.where(tcol < trow, 1.0, 0.0).astype(BF16)
    before = carry_ref[0:1, :] + jnp.dot(earlier, onehot.astype(BF16),
                                         preferred_element_type=F32)
    r0 = jnp.sum(jnp.where(hit0, before, 0.0), axis=-1, keepdims=True)
    r1 = jnp.sum(jnp.where(hit1, before, 0.0), axis=-1, keepdims=True)
    new_carry = carry_ref[0:1, :] + jnp.sum(onehot, axis=0, keepdims=True)
    carry_ref[0:1, :] = new_carry
    cnt_ref[...] = jnp.broadcast_to(new_carry, cnt_ref.shape)

    e0 = (i1 - ROUTER_EXPERT_LANE0).astype(F32)
    e1 = (i2 - ROUTER_EXPERT_LANE0).astype(F32)
    slab = jnp.zeros((POST_ROWS, LANES), F32)
    for pos, val in ((SLAB_E0, e0), (SLAB_E1, e1), (SLAB_R0, r0), (SLAB_R1, r1),
                     (SLAB_W0, w0), (SLAB_W1, w1)):
        slab = jnp.where(lane == pos, val, slab)
    slab_ref[...] = slab


def _post_attention(o_a, o_b, proj, x2, wua, wub, wo, gn, wr_hi_lo, br):
    n = x2.shape[0]
    const = lambda i: (0, 0)
    return pl.pallas_call(
        _post_kernel,
        grid=(n // POST_ROWS,),
        in_specs=[
            pl.BlockSpec((POST_ROWS, SB_WIDTH), lambda i: (i, 0)),
            pl.BlockSpec((POST_ROWS, DIFF_V_WIDTH), lambda i: (i, 0)),
            pl.BlockSpec((POST_ROWS, D_MODEL), lambda i: (i, GATE_A_B1024)),
            pl.BlockSpec((POST_ROWS, D_MODEL), lambda i: (i, GATE_B_B1024)),
            pl.BlockSpec((POST_ROWS, D_MODEL), lambda i: (i, 0)),
            pl.BlockSpec((SB_WIDTH, D_MODEL), const),
            pl.BlockSpec((DIFF_V_WIDTH, D_MODEL), const),
            pl.BlockSpec((D_MODEL, D_MODEL), const),
            pl.BlockSpec((1, D_MODEL), const),
            pl.BlockSpec((D_MODEL, 2 * LANES), const),
            pl.BlockSpec((1, LANES), const),
        ],
        out_specs=[
            pl.BlockSpec((POST_ROWS, D_MODEL), lambda i: (i, 0)),
            pl.BlockSpec((POST_ROWS * ROW_TILE, LANES), lambda i: (i, 0)),
            pl.BlockSpec((POST_ROWS, LANES), lambda i: (i, 0)),
            pl.BlockSpec((8, LANES), const),
        ],
        out_shape=[
            jax.ShapeDtypeStruct((n, D_MODEL), F32),
            jax.ShapeDtypeStruct((n * ROW_TILE, LANES), F32),
            jax.ShapeDtypeStruct((n, LANES), F32),
            jax.ShapeDtypeStruct((8, LANES), F32),
        ],
        scratch_shapes=[pltpu.VMEM((8, LANES), F32)],
        compiler_params=pltpu.CompilerParams(
            dimension_semantics=("arbitrary",), vmem_limit_bytes=VMEM_LIMIT),
        name="merge_route",
    )(o_a, o_b, proj, proj, x2, wua, wub, wo, gn, wr_hi_lo, br)


def _issue_row_copies(make_copy):
    def group(g, _):
        for u in range(ROW_DMA_UNROLL):
            for k in range(TOP_K):
                make_copy(g * ROW_DMA_UNROLL + u, k).start(priority=k % 2)
        return 0

    lax.fori_loop(0, ROW_DMA_TOKENS // ROW_DMA_UNROLL, group, 0)


def _scatter_kernel(dest_ref, pad_ref, h_ref, xd_ref, zero_tile, sem):
    n_pad = pad_ref.shape[-1]
    zero_tile[...] = jnp.zeros_like(zero_tile)
    _issue_row_copies(
        lambda t, k: _row_tile_copy(h_ref, t, xd_ref, dest_ref[0, 0, t * TOP_K + k], sem))
    for z in range(n_pad):
        _row_tile_copy(zero_tile, 0, xd_ref, pad_ref[0, 0, z], sem).start(priority=z % 2)
    for _ in range(TOP_K):
        pltpu.make_async_copy(
            h_ref, xd_ref.at[pl.ds(0, ROW_DMA_TOKENS * ROW_TILE), :], sem).wait()
    pltpu.make_async_copy(
        h_ref.at[pl.ds(0, n_pad * ROW_TILE), :], xd_ref.at[pl.ds(0, n_pad * ROW_TILE), :],
        sem).wait()


def _scatter_rows(dest3, pad3, h2_tiles, n_rows):
    n = h2_tiles.shape[0] // ROW_TILE
    steps = n // ROW_DMA_TOKENS
    assert pad3.shape[0] == steps and pad3.shape[2] <= ROW_DMA_TOKENS
    return pl.pallas_call(
        _scatter_kernel,
        grid=(steps,),
        in_specs=[
            pl.BlockSpec((1, 1, TOP_K * ROW_DMA_TOKENS), lambda i: (i, 0, 0),
                         memory_space=pltpu.SMEM),
            pl.BlockSpec((1, 1, pad3.shape[2]), lambda i: (i, 0, 0), memory_space=pltpu.SMEM),
            pl.BlockSpec((ROW_DMA_TOKENS * ROW_TILE, LANES), lambda i: (i, 0)),
        ],
        out_specs=pl.BlockSpec(memory_space=pl.ANY),
        out_shape=jax.ShapeDtypeStruct((n_rows * ROW_TILE, LANES), F32),
        scratch_shapes=[pltpu.VMEM((ROW_TILE, LANES), F32), pltpu.SemaphoreType.DMA],
        compiler_params=pltpu.CompilerParams(
            dimension_semantics=("arbitrary",), vmem_limit_bytes=VMEM_LIMIT),
        name="scatter_rows",
    )(dest3, pad3, h2_tiles)


def _expert_kernel(be_ref, nb_ref, x_ref, *refs):
    w_refs, y_ref = refs[:-1], refs[-1]
    first_block = pl.program_id(0) * MOE_GROUP

    @pl.when(first_block < nb_ref[0])
    def _():
        for blk in range(MOE_GROUP):
            wg_ref, wu_ref, wd_ref = w_refs[3 * blk:3 * blk + 3]
            x = _load_row_tiles(x_ref, MOE_BLOCK, first_row=blk * MOE_BLOCK).astype(BF16)
            g = jnp.dot(x, wg_ref[0], preferred_element_type=F32)
            u = jnp.dot(x, wu_ref[0], preferred_element_type=F32)
            hmid = (g * jax.nn.sigmoid(g) * u).astype(BF16)
            _store_row_tiles(y_ref, jnp.dot(hmid, wd_ref[0], preferred_element_type=F32),
                             first_row=blk * MOE_BLOCK)

    @pl.when(first_block >= nb_ref[0])
    def _():
        y_ref[...] = jnp.zeros_like(y_ref)


def _expert_mlp(block_e, n_used, xd_tiles, wg, wu, wd):
    n_blocks = xd_tiles.shape[0] // (MOE_BLOCK * ROW_TILE)
    assert n_blocks % MOE_GROUP == 0
    step_rows = MOE_GROUP * MOE_BLOCK * ROW_TILE

    def w_spec(shape, blk):
        return pl.BlockSpec((1,) + shape, lambda i, be, nb: (be[i * MOE_GROUP + blk], 0, 0))

    w_specs, w_args = [], []
    for blk in range(MOE_GROUP):
        w_specs += [w_spec((D_MODEL, D_EXPERT), blk), w_spec((D_MODEL, D_EXPERT), blk),
                    w_spec((D_EXPERT, D_MODEL), blk)]
        w_args += [wg, wu, wd]
    return pl.pallas_call(
        _expert_kernel,
        grid_spec=pltpu.PrefetchScalarGridSpec(
            num_scalar_prefetch=2,
            grid=(n_blocks // MOE_GROUP,),
            in_specs=[pl.BlockSpec((step_rows, LANES), lambda i, be, nb: (i, 0))] + w_specs,
            out_specs=pl.BlockSpec((step_rows, LANES), lambda i, be, nb: (i, 0)),
        ),
        out_shape=jax.ShapeDtypeStruct(xd_tiles.shape, F32),
        compiler_params=pltpu.CompilerParams(
            dimension_semantics=("arbitrary",), vmem_limit_bytes=VMEM_LIMIT),
        name="expert_mlp",
    )(block_e, n_used, xd_tiles, *w_args)


def _combine_kernel(dest_ref, next_dest_ref, x1_ref, slab_ref, g_ref, yd_ref, o_ref, ybuf, sems):
    step = pl.program_id(0)
    slot_rows = TOP_K * ROW_DMA_TOKENS

    def gather(idx_ref, slot):
        _issue_row_copies(
            lambda t, k: _row_tile_copy(yd_ref, idx_ref[0, 0, t * TOP_K + k], ybuf,
                                        slot * slot_rows + k * ROW_DMA_TOKENS + t,
                                        sems.at[slot]))

    @pl.when(step == 0)
    def _():
        gather(dest_ref, 0)

    @pl.when(step + 1 < pl.num_programs(0))
    def _():
        gather(next_dest_ref, (step + 1) % 2)

    slot = step % 2
    first = slot * slot_rows
    pltpu.make_async_copy(
        yd_ref.at[pl.ds(0, slot_rows * ROW_TILE), :],
        ybuf.at[pl.ds(pl.multiple_of(first * ROW_TILE, ROW_TILE), slot_rows * ROW_TILE), :],
        sems.at[slot]).wait()

    slab = slab_ref[...]
    w0 = slab[:, SLAB_W0:SLAB_W0 + 1]
    w1 = slab[:, SLAB_W1:SLAB_W1 + 1]
    x = (x1_ref[...] + w0 * _load_row_tiles(ybuf, ROW_DMA_TOKENS, first_row=first)
         + w1 * _load_row_tiles(ybuf, ROW_DMA_TOKENS, first_row=first + ROW_DMA_TOKENS))
    ms = jnp.mean(x * x, axis=-1, keepdims=True)
    o_ref[...] = x * lax.rsqrt(ms + EPS) * g_ref[...]


def _combine(dest3, x1, slab, g_final, yd_tiles):
    n = x1.shape[0]
    steps = n // ROW_DMA_TOKENS
    return pl.pallas_call(
        _combine_kernel,
        grid=(steps,),
        in_specs=[
            pl.BlockSpec((1, 1, TOP_K * ROW_DMA_TOKENS), lambda i: (i, 0, 0),
                         memory_space=pltpu.SMEM),
            pl.BlockSpec((1, 1, TOP_K * ROW_DMA_TOKENS),
                         lambda i: (jnp.minimum(i + 1, steps - 1), 0, 0),
                         memory_space=pltpu.SMEM),
            pl.BlockSpec((ROW_DMA_TOKENS, D_MODEL), lambda i: (i, 0)),
            pl.BlockSpec((ROW_DMA_TOKENS, LANES), lambda i: (i, 0)),
            pl.BlockSpec((1, D_MODEL), lambda i: (0, 0)),
            pl.BlockSpec(memory_space=pl.ANY),
        ],
        out_specs=pl.BlockSpec((ROW_DMA_TOKENS, D_MODEL), lambda i: (i, 0)),
        out_shape=jax.ShapeDtypeStruct((n, D_MODEL), F32),
        scratch_shapes=[pltpu.VMEM((2 * TOP_K * ROW_DMA_TOKENS * ROW_TILE, LANES), F32),
                        pltpu.SemaphoreType.DMA((2,))],
        compiler_params=pltpu.CompilerParams(
            dimension_semantics=("arbitrary",), vmem_limit_bytes=VMEM_LIMIT),
        name="combine_norm",
    )(dest3, dest3, x1, slab, g_final, yd_tiles)


def _rope_tables(seq):
    pos = jnp.arange(seq, dtype=F32)
    inv_freq = 1.0 / (ROPE_THETA ** (jnp.arange(0, DIFF_QK_DIM, 2, dtype=F32) / DIFF_QK_DIM))
    freqs = pos[:, None] * inv_freq[None, :]
    emb = jnp.concatenate([freqs, freqs], axis=-1)
    cos, sin = jnp.cos(emb), jnp.sin(emb)
    half = DIFF_QK_DIM // 2
    sin_signed = jnp.concatenate([-sin[:, :half], sin[:, half:]], axis=-1)
    reps = LANES // DIFF_QK_DIM
    return (jnp.tile(cos, (1, reps)), jnp.tile(sin_signed, (1, reps)), cos.T, sin_signed.T)


def _split_in_projection(w):
    edges = [0]
    for width in (SB_WIDTH, SB_WIDTH, SB_WIDTH, DIFF_QK_WIDTH, DIFF_QK_WIDTH, DIFF_V_WIDTH,
                  D_MODEL, D_MODEL):
        edges.append(edges[-1] + width)
    sb_q, sb_k, sb_v, d_q, d_k, d_v, gate_a, gate_b = (
        w[:, lo:hi] for lo, hi in zip(edges[:-1], edges[1:]))
    w_rows = jnp.concatenate([sb_k, d_k, gate_a, gate_b], axis=1)
    w_feat = jnp.concatenate([d_q, d_v], axis=1)
    w_feat_sb = jnp.concatenate([sb_q, sb_v], axis=1)
    return w_rows.astype(BF16), w_feat.astype(BF16), w_feat_sb.astype(BF16)


def _split_bf16(w):
    hi = w.astype(BF16)
    lo = (w - hi.astype(F32)).astype(BF16)
    return hi, lo


def kernel(x, g_norm_mix, w_in, lambda_q1, lambda_k1, lambda_q2, lambda_k2, g_subln,
           w_up_a, w_up_b, w_out, g_norm_ffn, w_router_group, b_router_group,
           w_router_expert, b_router_expert, w_expert_gate, w_expert_up, w_expert_down,
           g_norm_final):
    batch, seq, d = x.shape
    depth = w_in.shape[0]
    n_tok = batch * seq
    assert depth == 1, "the final RMSNorm is fused into the layer's combine step"
    assert d == D_MODEL and seq % DIFF_BLOCK == 0 and seq % PROJ_ROWS == 0
    assert seq % SB_BLOCK == 0 and PROJ_ROWS % SB_BLOCK == 0
    assert DIFF_BLOCK == PROJ_ROWS, "feature-major q/v blocks are written per projection step"
    assert (N_EXPERTS * MOE_BLOCK) % (n_tok // ROW_DMA_TOKENS) == 0
    assert n_tok % POST_ROWS == 0 and n_tok % ROW_DMA_TOKENS == 0
    n_assign = n_tok * TOP_K
    n_blocks = n_assign // MOE_BLOCK + N_EXPERTS
    n_rows = n_blocks * MOE_BLOCK

    rope = _rope_tables(seq)
    x2 = x.reshape(n_tok, d)
    for l in range(depth):
        lambda_init = 0.8 - 0.6 * math.exp(-0.3 * l)
        w_rows, w_feat, w_feat_sb = _split_in_projection(w_in[l])
        proj, proj_t, proj_ts = _in_projection(x2, g_norm_mix[l][None, :], w_rows, w_feat,
                                               w_feat_sb, rope, seq)
        o_a = _sb_attention(proj, proj_ts, batch, seq)
        lam_rows = jnp.stack([lambda_q1[l], lambda_k1[l], lambda_q2[l], lambda_k2[l]])
        o_b = _diff_attention(proj, proj_t, lam_rows, g_subln[l][:, None], batch, seq,
                              lambda_init)

        w_r = jnp.zeros((d, LANES), F32)
        w_r = w_r.at[:, :N_GROUPS].set(w_router_group[l])
        w_r = w_r.at[:, N_GROUPS:N_GROUPS + N_EXPERTS].set(w_router_expert[l])
        b_r = jnp.zeros((1, LANES), F32)
        b_r = b_r.at[0, :N_GROUPS].set(b_router_group[l])
        b_r = b_r.at[0, N_GROUPS:N_GROUPS + N_EXPERTS].set(b_router_expert[l])
        wr_hi_lo = jnp.concatenate(_split_bf16(w_r), axis=1)

        x1, h2, slab, cnt = _post_attention(
            o_a, o_b, proj, x2, w_up_a[l].astype(BF16), w_up_b[l].astype(BF16),
            w_out[l].astype(BF16), g_norm_ffn[l][None, :], wr_hi_lo, b_r)

        counts = cnt[0, N_GROUPS:N_GROUPS + N_EXPERTS].astype(jnp.int32)
        padded = (counts + MOE_BLOCK - 1) // MOE_BLOCK * MOE_BLOCK
        pad_end = jnp.cumsum(padded)
        pad_start = pad_end - padded
        e_ids = slab[:, SLAB_E0:SLAB_E1 + 1].astype(jnp.int32)
        ranks = slab[:, SLAB_R0:SLAB_R1 + 1].astype(jnp.int32)
        expert_iota = jnp.arange(N_EXPERTS, dtype=jnp.int32)
        dest = ranks + jnp.sum(
            jnp.where(e_ids[:, :, None] == expert_iota, pad_start, 0), axis=-1)
        steps = n_tok // ROW_DMA_TOKENS
        dest3 = dest.reshape(steps, 1, TOP_K * ROW_DMA_TOKENS)
        gap = padded - counts
        gap_end = jnp.cumsum(gap)
        z = jnp.arange(n_rows - n_assign, dtype=jnp.int32)
        seg = jnp.sum((gap_end[None, :] <= z[:, None]).astype(jnp.int32), axis=1)
        seg_onehot = seg[:, None] == jnp.arange(N_EXPERTS + 1, dtype=jnp.int32)
        seg_first_row = jnp.concatenate([pad_start + counts, pad_end[-1:]])
        seg_first_z = jnp.concatenate([gap_end - gap, gap_end[-1:]])
        pad_rows = z + jnp.sum(jnp.where(seg_onehot, seg_first_row - seg_first_z, 0), axis=1)
        pad3 = pad_rows.reshape(steps, 1, (n_rows - n_assign) // steps)
        block_start = jnp.arange(n_blocks, dtype=jnp.int32) * MOE_BLOCK
        block_e = jnp.minimum(
            jnp.sum((pad_end[None, :] <= block_start[:, None]).astype(jnp.int32), axis=1),
            N_EXPERTS - 1)
        n_used = (pad_end[-1:] // MOE_BLOCK).astype(jnp.int32)

        xd = _scatter_rows(dest3, pad3, h2, n_rows)
        yd = _expert_mlp(block_e, n_used, xd, w_expert_gate[l].astype(BF16),
                         w_expert_up[l].astype(BF16), w_expert_down[l].astype(BF16))
        x2 = _combine(dest3, x1, slab, g_norm_final[None, :], yd)
    return x2.reshape(batch, seq, d)
```

```python
import functools
import math

import jax
import jax.numpy as jnp
from jax import lax
from jax.experimental import pallas as pl
from jax.experimental.pallas import tpu as pltpu

D_MODEL = 1024
HEAD_DIM = 64
SB_HEADS = 8
SB_WIDTH = SB_HEADS * HEAD_DIM
DIFF_HEADS = 4
DIFF_QK_DIM = 64
DIFF_V_DIM = 2 * DIFF_QK_DIM
DIFF_QK_WIDTH = DIFF_HEADS * 2 * DIFF_QK_DIM
DIFF_V_WIDTH = DIFF_HEADS * DIFF_V_DIM
IN_WIDTH = 3 * SB_WIDTH + 2 * DIFF_QK_WIDTH + DIFF_V_WIDTH + 2 * D_MODEL
ROPE_THETA = 10000.0
N_GROUPS = 4
EXPERTS_PER_GROUP = 8
N_EXPERTS = N_GROUPS * EXPERTS_PER_GROUP
TOP_K = 2
D_EXPERT = 256
EPS = 1e-6

LANES = 128
SUBLANES = 8
QK_SCALE = HEAD_DIM ** -0.5
LOG2E = math.log2(math.e)

F32 = jnp.float32
BF16 = jnp.bfloat16

PROJ_ROWS = 512
PROJ_CHUNK = 512
SB_BLOCK = 256
DIFF_BLOCK = 512
SB_MIN_WEIGHT = 2.0 ** -126
SB_MAX_LOG2 = 126.0
POST_ROWS = 512
MOE_BLOCK = 256
MOE_GROUP = 2
ROW_DMA_TOKENS = 512
ROW_DMA_UNROLL = 8
VMEM_LIMIT = 56 * 1024 * 1024

SB_RUN = SB_BLOCK // SUBLANES
ROW_WIDTH = SB_WIDTH + DIFF_QK_WIDTH + 2 * D_MODEL
T_WIDTH = DIFF_QK_WIDTH + DIFF_V_WIDTH
TS_WIDTH = 2 * SB_WIDTH
SBK_CB, DK_CB = 0, 4
GATE_A_B1024, GATE_B_B1024 = 1, 2
DQ_RB, DV_RB = 0, DIFF_QK_WIDTH // LANES
SBQ_RB, SBV_RB = 0, SB_WIDTH // LANES

SLAB_E0, SLAB_E1, SLAB_R0, SLAB_R1, SLAB_W0, SLAB_W1 = 0, 1, 2, 3, 4, 5
ROUTER_EXPERT_LANE0 = N_GROUPS


def _nt_dot(a, b):
    return lax.dot_general(a, b, (((1,), (1,)), ((), ())), preferred_element_type=F32)


def _tnt_dot(a, b):
    return lax.dot_general(a, b, (((0,), (1,)), ((), ())), preferred_element_type=F32)


ROW_TILE = D_MODEL // LANES


def _store_row_tiles(ref, value, first_row=0):
    rows = value.shape[0]
    for k in range(ROW_TILE):
        ref[pl.ds(first_row * ROW_TILE + k, rows, stride=ROW_TILE), :] = (
            value[:, k * LANES:(k + 1) * LANES])


def _load_row_tiles(ref, rows, first_row=0):
    return jnp.concatenate(
        [ref[pl.ds(first_row * ROW_TILE + k, rows, stride=ROW_TILE), :]
         for k in range(ROW_TILE)], axis=1)


def _row_tile_copy(src_ref, src_row, dst_ref, dst_row, sem):
    src = src_ref.at[pl.ds(pl.multiple_of(src_row * ROW_TILE, ROW_TILE), ROW_TILE), :]
    dst = dst_ref.at[pl.ds(pl.multiple_of(dst_row * ROW_TILE, ROW_TILE), ROW_TILE), :]
    return pltpu.make_async_copy(src, dst, sem)


def _inproj_kernel(x_ref, g_ref, w_ref, wt_ref, wts_ref, cos_ref, sin_ref, cost_ref, sint_ref,
                   o_ref, ot_ref, ots_ref):
    def normed(x):
        ms = jnp.mean(x * x, axis=-1, keepdims=True)
        return (x * lax.rsqrt(ms + EPS) * g_ref[...]).astype(BF16)

    h = normed(x_ref[...])
    dst = lax.broadcasted_iota(jnp.int32, (SB_BLOCK, SB_BLOCK), 0)
    src = lax.broadcasted_iota(jnp.int32, (SB_BLOCK, SB_BLOCK), 1)
    perm = jnp.where(src == (dst % SUBLANES) * SB_RUN + dst // SUBLANES, 1.0, 0.0).astype(BF16)
    h_perm = jnp.concatenate(
        [jnp.dot(perm, h[base:base + SB_BLOCK], preferred_element_type=F32).astype(BF16)
         for base in range(0, PROJ_ROWS, SB_BLOCK)], axis=0)

    lane = lax.broadcasted_iota(jnp.int32, (PROJ_ROWS, LANES), 1)
    first_half = (lane % DIFF_QK_DIM) < (DIFF_QK_DIM // 2)
    cos = cos_ref[...]
    sin = sin_ref[...]

    def rope(acc):
        outs = []
        for s in range(PROJ_CHUNK // LANES):
            a = acc[:, s * LANES:(s + 1) * LANES]
            swapped = jnp.where(first_half,
                                pltpu.roll(a, LANES - DIFF_QK_DIM // 2, 1),
                                pltpu.roll(a, DIFF_QK_DIM // 2, 1))
            outs.append(a * cos + swapped * sin)
        return jnp.concatenate(outs, axis=1)

    for c in range(ROW_WIDTH // PROJ_CHUNK):
        lo = c * PROJ_CHUNK
        lhs = h_perm if lo == SBK_CB * LANES else h
        acc = jnp.dot(lhs, w_ref[:, lo:lo + PROJ_CHUNK], preferred_element_type=F32)
        if lo == DK_CB * LANES:
            acc = rope(acc)
        o_ref[:, lo:lo + PROJ_CHUNK] = acc.astype(BF16)

    qt = _tnt_dot(wt_ref[:, 0:DIFF_QK_WIDTH], h)
    cos_f = cost_ref[...]
    sin_f = sint_ref[...]
    half = DIFF_QK_DIM // 2
    groups = []
    for gidx in range(DIFF_QK_WIDTH // DIFF_QK_DIM):
        xg = qt[gidx * DIFF_QK_DIM:(gidx + 1) * DIFF_QK_DIM]
        swapped = jnp.concatenate([xg[half:], xg[:half]], axis=0)
        groups.append((xg * cos_f + swapped * sin_f) * (QK_SCALE * LOG2E))
    ot_ref[0, 0:DIFF_QK_WIDTH, :] = jnp.concatenate(groups, axis=0).astype(BF16)
    ot_ref[0, DIFF_QK_WIDTH:T_WIDTH, :] = _tnt_dot(wt_ref[:, DIFF_QK_WIDTH:T_WIDTH],
                                                   h).astype(BF16)

    sq = (_tnt_dot(wts_ref[:, 0:SB_WIDTH], h) * (QK_SCALE * LOG2E)).astype(BF16)
    sv = _tnt_dot(wts_ref[:, SB_WIDTH:TS_WIDTH], h_perm).astype(BF16)
    for blk in range(PROJ_ROWS // SB_BLOCK):
        cols = slice(blk * SB_BLOCK, (blk + 1) * SB_BLOCK)
        ots_ref[blk, 0:SB_WIDTH, :] = sq[:, cols]
        ots_ref[blk, SB_WIDTH:TS_WIDTH, :] = sv[:, cols]


def _in_projection(x2, g, w_rows, w_feat, w_feat_sb, rope, seq):
    n = x2.shape[0]
    pos_blocks = seq // PROJ_ROWS
    sb_per_step = PROJ_ROWS // SB_BLOCK
    cos_t, sin_t, cos_f, sin_f = rope
    const = lambda i: (0, 0)
    return pl.pallas_call(
        _inproj_kernel,
        grid=(n // PROJ_ROWS,),
        in_specs=[
            pl.BlockSpec((PROJ_ROWS, D_MODEL), lambda i: (i, 0)),
            pl.BlockSpec((1, D_MODEL), const),
            pl.BlockSpec((D_MODEL, ROW_WIDTH), const),
            pl.BlockSpec((D_MODEL, T_WIDTH), const),
            pl.BlockSpec((D_MODEL, TS_WIDTH), const),
            pl.BlockSpec((PROJ_ROWS, LANES), lambda i: (i % pos_blocks, 0)),
            pl.BlockSpec((PROJ_ROWS, LANES), lambda i: (i % pos_blocks, 0)),
            pl.BlockSpec((DIFF_QK_DIM, PROJ_ROWS), lambda i: (0, i % pos_blocks)),
            pl.BlockSpec((DIFF_QK_DIM, PROJ_ROWS), lambda i: (0, i % pos_blocks)),
        ],
        out_specs=[
            pl.BlockSpec((PROJ_ROWS, ROW_WIDTH), lambda i: (i, 0)),
            pl.BlockSpec((1, T_WIDTH, PROJ_ROWS), lambda i: (i, 0, 0)),
            pl.BlockSpec((sb_per_step, TS_WIDTH, SB_BLOCK), lambda i: (i, 0, 0)),
        ],
        out_shape=[
            jax.ShapeDtypeStruct((n, ROW_WIDTH), BF16),
            jax.ShapeDtypeStruct((n // PROJ_ROWS, T_WIDTH, PROJ_ROWS), BF16),
            jax.ShapeDtypeStruct((n // SB_BLOCK, TS_WIDTH, SB_BLOCK), BF16),
        ],
        compiler_params=pltpu.CompilerParams(
            dimension_semantics=("parallel",), vmem_limit_bytes=VMEM_LIMIT),
        name="in_projection",
    )(x2, g, w_rows, w_feat, w_feat_sb, cos_t, sin_t, cos_f, sin_f)


def _sb_kernel(qt_ref, k_ref, vt_ref, o_ref):
    i = pl.program_id(1)
    n_pairs = SB_WIDTH // LANES
    feat = lax.broadcasted_iota(jnp.int32, (LANES, SB_BLOCK), 0)
    pos = lax.broadcasted_iota(jnp.int32, (SB_BLOCK, SB_BLOCK), 0)
    qry = lax.broadcasted_iota(jnp.int32, (SB_BLOCK, SB_BLOCK), 1)
    key = (pos % SUBLANES) * SB_RUN + pos // SUBLANES
    strict = key < qry
    first_head = feat < HEAD_DIM

    def head_queries(p):
        qt = qt_ref[0, p * LANES:(p + 1) * LANES, :]
        zero = jnp.zeros_like(qt)
        return jnp.where(first_head, qt, zero), jnp.where(first_head, zero, qt)

    qts = [q for p in range(n_pairs) for q in head_queries(p)]

    def pair_lanes(head):
        return slice((head // 2) * LANES, (head // 2 + 1) * LANES)

    def scores(j, head):
        start = pl.multiple_of(j * SB_BLOCK, SB_BLOCK)
        return jnp.dot(k_ref[pl.ds(start, SB_BLOCK), pair_lanes(head)], qts[head],
                       preferred_element_type=F32)

    def block(j, head, z, run, masked):
        e = jnp.exp2(jnp.minimum(z, SB_MAX_LOG2))
        keep = 1.0 / (1.0 + e)
        beta = e * keep
        if masked:
            keep = jnp.where(strict, keep, 1.0)
            beta = jnp.where(strict, beta, 0.0)
        keep3 = keep.reshape(SB_RUN, SUBLANES, SB_BLOCK)
        beta3 = beta.reshape(SB_RUN, SUBLANES, SB_BLOCK)
        after = [None] * SB_RUN
        acc = jnp.ones((SUBLANES, SB_BLOCK), F32)
        for v in range(SB_RUN - 1, -1, -1):
            after[v] = acc
            acc = acc * keep3[v]
        later = run
        offs = [None] * SUBLANES
        for r in range(SUBLANES - 1, -1, -1):
            offs[r] = later
            later = later * acc[r:r + 1]
        base = jnp.concatenate(offs, axis=0)
        a = jnp.concatenate([beta3[v] * after[v] * base for v in range(SB_RUN)], axis=0)
        pv = jnp.dot(vt_ref[j, pair_lanes(head), :], a.astype(BF16),
                     preferred_element_type=F32)
        return pv, later

    def pair_rows(pv_first, pv_second):
        return jnp.where(first_head, pv_first, pv_second)

    has_prev = i >= 1
    prev = jnp.maximum(i - 1, 0)
    z_diag = [scores(i, head) for head in range(SB_HEADS)]
    z_prev = [scores(prev, head) for head in range(SB_HEADS)]
    runs, pvs = [], []
    for head in range(SB_HEADS):
        pv_d, run_d = block(i, head, z_diag[head], jnp.ones((1, SB_BLOCK), F32), True)
        pv_p, run_p = block(prev, head, z_prev[head], run_d, False)
        pvs.append(pv_d + jnp.where(has_prev, pv_p, 0.0))
        runs.append(run_p)
    accs = [pair_rows(pvs[2 * p], pvs[2 * p + 1]) for p in range(n_pairs)]

    def alive(rs):
        top = rs[0]
        for r in rs[1:]:
            top = jnp.maximum(top, r)
        return (jnp.max(top) >= SB_MIN_WEIGHT).astype(jnp.int32)

    def cond(c):
        return (c[0] >= 0) & (c[1] > 0)

    def body(c):
        j, old_runs, old_accs = c[0], c[2:2 + SB_HEADS], c[2 + SB_HEADS:]
        new = [block(j, head, scores(j, head), old_runs[head], False)
               for head in range(SB_HEADS)]
        new_runs = [run for _, run in new]
        new_accs = [old_accs[p] + pair_rows(new[2 * p][0], new[2 * p + 1][0])
                    for p in range(n_pairs)]
        return (j - 1, alive(new_runs), *new_runs, *new_accs)

    out = lax.while_loop(cond, body, (i - 2, alive(runs), *runs, *accs))
    for p in range(n_pairs):
        o_ref[:, p * LANES:(p + 1) * LANES] = out[2 + SB_HEADS + p].T.astype(o_ref.dtype)


def _sb_attention(proj, proj_ts, batch, seq):
    nq = seq // SB_BLOCK
    q_rb, k_cb, v_rb = (SBQ_RB * LANES // SB_WIDTH, SBK_CB * LANES // SB_WIDTH,
                        SBV_RB * LANES // SB_WIDTH)
    return pl.pallas_call(
        _sb_kernel,
        grid=(batch, nq),
        in_specs=[
            pl.BlockSpec((1, SB_WIDTH, SB_BLOCK), lambda b, i: (b * nq + i, q_rb, 0)),
            pl.BlockSpec((seq, SB_WIDTH), lambda b, i: (b, k_cb)),
            pl.BlockSpec((nq, SB_WIDTH, SB_BLOCK), lambda b, i: (b, v_rb, 0)),
        ],
        out_specs=pl.BlockSpec((SB_BLOCK, SB_WIDTH), lambda b, i: (b * nq + i, 0)),
        out_shape=jax.ShapeDtypeStruct((batch * seq, SB_WIDTH), BF16),
        compiler_params=pltpu.CompilerParams(
            dimension_semantics=("parallel", "arbitrary"),
            vmem_limit_bytes=VMEM_LIMIT),
        name="stickbreak_attention",
    )(proj_ts, proj, proj_ts)


def _diff_kernel(lam_ref, gs_ref, qt_ref, k_ref, vt_ref, o_ref,
                 acc_ref, m_ref, l_ref, sa_ref, sb_ref, *, lambda_init):
    i = pl.program_id(1)
    feat = lax.broadcasted_iota(jnp.int32, (LANES, DIFF_BLOCK), 0)
    key = lax.broadcasted_iota(jnp.int32, (DIFF_BLOCK, DIFF_BLOCK), 0)
    qry = lax.broadcasted_iota(jnp.int32, (DIFF_BLOCK, DIFF_BLOCK), 1)
    causal = key <= qry
    first_comp = feat < DIFF_QK_DIM

    def head_queries(h):
        qt = qt_ref[0, h * LANES:(h + 1) * LANES, :]
        zero = jnp.zeros_like(qt)
        return jnp.where(first_comp, qt, zero), jnp.where(first_comp, zero, qt)

    qts = [q for h in range(DIFF_HEADS) for q in head_queries(h)]

    lv = lam_ref[...]
    lam = (jnp.exp(jnp.sum(lv[0:1] * lv[1:2], axis=-1, keepdims=True))
           - jnp.exp(jnp.sum(lv[2:3] * lv[3:4], axis=-1, keepdims=True)) + lambda_init)

    def head_lanes(h):
        return slice(h * LANES, (h + 1) * LANES)

    def scores(j, s_ref):
        start = pl.multiple_of(j * DIFF_BLOCK, DIFF_BLOCK)
        for h in range(DIFF_HEADS):
            kb = k_ref[pl.ds(start, DIFF_BLOCK), head_lanes(h)]
            for c in range(2):
                s_ref[2 * h + c] = jnp.dot(kb, qts[2 * h + c], preferred_element_type=F32)

    def absorb(j, s_ref, masked):
        for h in range(DIFF_HEADS):
            vt = vt_ref[j, head_lanes(h), :]
            for c in range(2):
                n = 2 * h + c
                m = m_ref[n]
                s = s_ref[n]
                if masked:
                    s = jnp.where(causal, s, -jnp.inf)
                m_new = jnp.maximum(m, jnp.max(s, axis=0, keepdims=True))
                alpha = jnp.exp2(m - m_new)
                p = jnp.exp2(s - m_new)
                m_ref[n] = m_new
                l_ref[n] = alpha * l_ref[n] + jnp.sum(p, axis=0, keepdims=True)
                acc_ref[n] = alpha * acc_ref[n] + jnp.dot(vt, p.astype(BF16),
                                                          preferred_element_type=F32)

    acc_ref[...] = jnp.zeros_like(acc_ref)
    l_ref[...] = jnp.zeros_like(l_ref)
    m_ref[...] = jnp.full(m_ref.shape, -jnp.inf, F32)

    scores(0, sa_ref)

    def pair(t, _):
        b = 2 * t
        scores(b + 1, sb_ref)
        absorb(b, sa_ref, False)
        scores(b + 2, sa_ref)
        absorb(b + 1, sb_ref, False)
        return 0

    lax.fori_loop(0, i // 2, pair, 0)

    @pl.when(i % 2 == 0)
    def _():
        absorb(i, sa_ref, True)

    @pl.when(i % 2 == 1)
    def _():
        scores(i, sb_ref)
        absorb(i - 1, sa_ref, False)
        absorb(i, sb_ref, True)

    for h in range(DIFF_HEADS):
        ot = (acc_ref[2 * h] / l_ref[2 * h]
              - lam * (acc_ref[2 * h + 1] / l_ref[2 * h + 1]))
        ms = jnp.mean(ot * ot, axis=0, keepdims=True)
        ot = ot * lax.rsqrt(ms + EPS) * gs_ref[...] * (1.0 - lambda_init)
        o_ref[:, head_lanes(h)] = ot.T.astype(o_ref.dtype)


def _diff_attention(proj, proj_t, lam_rows, g_subln_col, batch, seq, lambda_init):
    nq = seq // DIFF_BLOCK
    n_chains = 2 * DIFF_HEADS
    q_rb, k_cb, v_rb = (DQ_RB * LANES // DIFF_V_WIDTH, DK_CB * LANES // DIFF_V_WIDTH,
                        DV_RB * LANES // DIFF_V_WIDTH)
    return pl.pallas_call(
        functools.partial(_diff_kernel, lambda_init=lambda_init),
        grid=(batch, nq),
        in_specs=[
            pl.BlockSpec((4, DIFF_QK_DIM), lambda b, i: (0, 0)),
            pl.BlockSpec((DIFF_V_DIM, 1), lambda b, i: (0, 0)),
            pl.BlockSpec((1, DIFF_QK_WIDTH, DIFF_BLOCK), lambda b, i: (b * nq + i, q_rb, 0)),
            pl.BlockSpec((seq, DIFF_QK_WIDTH), lambda b, i: (b, k_cb)),
            pl.BlockSpec((nq, DIFF_V_WIDTH, DIFF_BLOCK), lambda b, i: (b, v_rb, 0)),
        ],
        out_specs=pl.BlockSpec((DIFF_BLOCK, DIFF_V_WIDTH), lambda b, i: (b * nq + i, 0)),
        out_shape=jax.ShapeDtypeStruct((batch * seq, DIFF_V_WIDTH), BF16),
        scratch_shapes=[pltpu.VMEM((n_chains, DIFF_V_DIM, DIFF_BLOCK), F32),
                        pltpu.VMEM((n_chains, 1, DIFF_BLOCK), F32),
                        pltpu.VMEM((n_chains, 1, DIFF_BLOCK), F32),
                        pltpu.VMEM((n_chains, DIFF_BLOCK, DIFF_BLOCK), F32),
                        pltpu.VMEM((n_chains, DIFF_BLOCK, DIFF_BLOCK), F32)],
        compiler_params=pltpu.CompilerParams(
            dimension_semantics=("parallel", "arbitrary"),
            vmem_limit_bytes=VMEM_LIMIT),
        name="differential_attention",
    )(lam_rows, g_subln_col, proj_t, proj, proj_t)


def _post_kernel(oa_ref, ob_ref, ga_ref, gb_ref, x_ref, wua_ref, wub_ref, wo_ref, gn_ref,
                 wr_ref, br_ref,
                 x1_ref, h2_ref, slab_ref, cnt_ref, carry_ref):
    step_id = pl.program_id(0)

    @pl.when(step_id == 0)
    def _():
        carry_ref[...] = jnp.zeros_like(carry_ref)

    u_a = jnp.dot(oa_ref[...], wua_ref[...], preferred_element_type=F32)
    u_b = jnp.dot(ob_ref[...], wub_ref[...], preferred_element_type=F32)
    y = (jax.nn.sigmoid(ga_ref[...].astype(F32)) * u_a
         + jax.nn.sigmoid(gb_ref[...].astype(F32)) * u_b)
    x1 = x_ref[...] + jnp.dot(y.astype(BF16), wo_ref[...], preferred_element_type=F32)
    x1_ref[...] = x1
    ms = jnp.mean(x1 * x1, axis=-1, keepdims=True)
    h2 = x1 * lax.rsqrt(ms + EPS) * gn_ref[...]
    _store_row_tiles(h2_ref, h2)

    h_hi = h2.astype(BF16)
    h_lo = (h2 - h_hi.astype(F32)).astype(BF16)
    hi_terms = jnp.dot(h_hi, wr_ref[...], preferred_element_type=F32)
    logits = (hi_terms[:, :LANES] + hi_terms[:, LANES:]
              + jnp.dot(h_lo, wr_ref[:, :LANES], preferred_element_type=F32)
              + br_ref[...])

    lane = lax.broadcasted_iota(jnp.int32, (POST_ROWS, LANES), 1)
    neg = -jnp.inf

    def first_argmax(v):
        m = jnp.max(v, axis=-1, keepdims=True)
        idx = jnp.min(jnp.where(v == m, lane, LANES), axis=-1, keepdims=True)
        return m, idx

    g_logits = jnp.where(lane < N_GROUPS, logits, neg)
    g_max, grp = first_argmax(g_logits)
    p_grp = 1.0 / jnp.sum(jnp.exp(g_logits - g_max), axis=-1, keepdims=True)

    lo_lane = ROUTER_EXPERT_LANE0 + grp * EXPERTS_PER_GROUP
    in_grp = (lane >= lo_lane) & (lane < lo_lane + EXPERTS_PER_GROUP)
    e_logits = jnp.where(in_grp, logits, neg)
    m1, i1 = first_argmax(e_logits)
    m2, i2 = first_argmax(jnp.where(lane == i1, neg, e_logits))
    r = jnp.exp(m2 - m1)
    w0 = p_grp / (1.0 + r)
    w1 = p_grp * r / (1.0 + r)

    hit0 = lane == i1
    hit1 = lane == i2
    onehot = jnp.where(hit0 | hit1, 1.0, 0.0)
    trow = lax.broadcasted_iota(jnp.int32, (POST_ROWS, POST_ROWS), 0)
    tcol = lax.broadcasted_iota(jnp.int32, (POST_ROWS, POST_ROWS), 1)
    earlier = jnp.where(tcol < trow, 1.0, 0.0).astype(BF16)
    before = carry_ref[0:1, :] + jnp.dot(earlier, onehot.astype(BF16),
                                         preferred_element_type=F32)
    r0 = jnp.sum(jnp.where(hit0, before, 0.0), axis=-1, keepdims=True)
    r1 = jnp.sum(jnp.where(hit1, before, 0.0), axis=-1, keepdims=True)
    new_carry = carry_ref[0:1, :] + jnp.sum(onehot, axis=0, keepdims=True)
    carry_ref[0:1, :] = new_carry
    cnt_ref[...] = jnp.broadcast_to(new_carry, cnt_ref.shape)

    e0 = (i1 - ROUTER_EXPERT_LANE0).astype(F32)
    e1 = (i2 - ROUTER_EXPERT_LANE0).astype(F32)
    slab = jnp.zeros((POST_ROWS, LANES), F32)
    for pos, val in ((SLAB_E0, e0), (SLAB_E1, e1), (SLAB_R0, r0), (SLAB_R1, r1),
                     (SLAB_W0, w0), (SLAB_W1, w1)):
        slab = jnp.where(lane == pos, val, slab)
    slab_ref[...] = slab


def _post_attention(o_a, o_b, proj, x2, wua, wub, wo, gn, wr_hi_lo, br):
    n = x2.shape[0]
    const = lambda i: (0, 0)
    return pl.pallas_call(
        _post_kernel,
        grid=(n // POST_ROWS,),
        in_specs=[
            pl.BlockSpec((POST_ROWS, SB_WIDTH), lambda i: (i, 0)),
            pl.BlockSpec((POST_ROWS, DIFF_V_WIDTH), lambda i: (i, 0)),
            pl.BlockSpec((POST_ROWS, D_MODEL), lambda i: (i, GATE_A_B1024)),
            pl.BlockSpec((POST_ROWS, D_MODEL), lambda i: (i, GATE_B_B1024)),
            pl.BlockSpec((POST_ROWS, D_MODEL), lambda i: (i, 0)),
            pl.BlockSpec((SB_WIDTH, D_MODEL), const),
            pl.BlockSpec((DIFF_V_WIDTH, D_MODEL), const),
            pl.BlockSpec((D_MODEL, D_MODEL), const),
            pl.BlockSpec((1, D_MODEL), const),
            pl.BlockSpec((D_MODEL, 2 * LANES), const),
            pl.BlockSpec((1, LANES), const),
        ],
        out_specs=[
            pl.BlockSpec((POST_ROWS, D_MODEL), lambda i: (i, 0)),
            pl.BlockSpec((POST_ROWS * ROW_TILE, LANES), lambda i: (i, 0)),
            pl.BlockSpec((POST_ROWS, LANES), lambda i: (i, 0)),
            pl.BlockSpec((8, LANES), const),
        ],
        out_shape=[
            jax.ShapeDtypeStruct((n, D_MODEL), F32),
            jax.ShapeDtypeStruct((n * ROW_TILE, LANES), F32),
            jax.ShapeDtypeStruct((n, LANES), F32),
            jax.ShapeDtypeStruct((8, LANES), F32),
        ],
        scratch_shapes=[pltpu.VMEM((8, LANES), F32)],
        compiler_params=pltpu.CompilerParams(
            dimension_semantics=("arbitrary",), vmem_limit_bytes=VMEM_LIMIT),
        name="merge_route",
    )(o_a, o_b, proj, proj, x2, wua, wub, wo, gn, wr_hi_lo, br)


def _issue_row_copies(make_copy):
    def group(g, _):
        for u in range(ROW_DMA_UNROLL):
            for k in range(TOP_K):
                make_copy(g * ROW_DMA_UNROLL + u, k).start(priority=k % 2)
        return 0

    lax.fori_loop(0, ROW_DMA_TOKENS // ROW_DMA_UNROLL, group, 0)


def _scatter_kernel(dest_ref, pad_ref, h_ref, xd_ref, zero_tile, sem):
    n_pad = pad_ref.shape[-1]
    zero_tile[...] = jnp.zeros_like(zero_tile)
    _issue_row_copies(
        lambda t, k: _row_tile_copy(h_ref, t, xd_ref, dest_ref[0, 0, t * TOP_K + k], sem))
    for z in range(n_pad):
        _row_tile_copy(zero_tile, 0, xd_ref, pad_ref[0, 0, z], sem).start(priority=z % 2)
    for _ in range(TOP_K):
        pltpu.make_async_copy(
            h_ref, xd_ref.at[pl.ds(0, ROW_DMA_TOKENS * ROW_TILE), :], sem).wait()
    pltpu.make_async_copy(
        h_ref.at[pl.ds(0, n_pad * ROW_TILE), :], xd_ref.at[pl.ds(0, n_pad * ROW_TILE), :],
        sem).wait()


def _scatter_rows(dest3, pad3, h2_tiles, n_rows):
    n = h2_tiles.shape[0] // ROW_TILE
    steps = n // ROW_DMA_TOKENS
    assert pad3.shape[0] == steps and pad3.shape[2] <= ROW_DMA_TOKENS
    return pl.pallas_call(
        _scatter_kernel,
        grid=(steps,),
        in_specs=[
            pl.BlockSpec((1, 1, TOP_K * ROW_DMA_TOKENS), lambda i: (i, 0, 0),
                         memory_space=pltpu.SMEM),
            pl.BlockSpec((1, 1, pad3.shape[2]), lambda i: (i, 0, 0), memory_space=pltpu.SMEM),
            pl.BlockSpec((ROW_DMA_TOKENS * ROW_TILE, LANES), lambda i: (i, 0)),
        ],
        out_specs=pl.BlockSpec(memory_space=pl.ANY),
        out_shape=jax.ShapeDtypeStruct((n_rows * ROW_TILE, LANES), F32),
        scratch_shapes=[pltpu.VMEM((ROW_TILE, LANES), F32), pltpu.SemaphoreType.DMA],
        compiler_params=pltpu.CompilerParams(
            dimension_semantics=("arbitrary",), vmem_limit_bytes=VMEM_LIMIT),
        name="scatter_rows",
    )(dest3, pad3, h2_tiles)


def _expert_kernel(be_ref, nb_ref, x_ref, *refs):
    w_refs, y_ref = refs[:-1], refs[-1]
    first_block = pl.program_id(0) * MOE_GROUP

    @pl.when(first_block < nb_ref[0])
    def _():
        for blk in range(MOE_GROUP):
            wg_ref, wu_ref, wd_ref = w_refs[3 * blk:3 * blk + 3]
            x = _load_row_tiles(x_ref, MOE_BLOCK, first_row=blk * MOE_BLOCK).astype(BF16)
            g = jnp.dot(x, wg_ref[0], preferred_element_type=F32)
            u = jnp.dot(x, wu_ref[0], preferred_element_type=F32)
            hmid = (g * jax.nn.sigmoid(g) * u).astype(BF16)
            _store_row_tiles(y_ref, jnp.dot(hmid, wd_ref[0], preferred_element_type=F32),
                             first_row=blk * MOE_BLOCK)

    @pl.when(first_block >= nb_ref[0])
    def _():
        y_ref[...] = jnp.zeros_like(y_ref)


def _expert_mlp(block_e, n_used, xd_tiles, wg, wu, wd):
    n_blocks = xd_tiles.shape[0] // (MOE_BLOCK * ROW_TILE)
    assert n_blocks % MOE_GROUP == 0
    step_rows = MOE_GROUP * MOE_BLOCK * ROW_TILE

    def w_spec(shape, blk):
        return pl.BlockSpec((1,) + shape, lambda i, be, nb: (be[i * MOE_GROUP + blk], 0, 0))

    w_specs, w_args = [], []
    for blk in range(MOE_GROUP):
        w_specs += [w_spec((D_MODEL, D_EXPERT), blk), w_spec((D_MODEL, D_EXPERT), blk),
                    w_spec((D_EXPERT, D_MODEL), blk)]
        w_args += [wg, wu, wd]
    return pl.pallas_call(
        _expert_kernel,
        grid_spec=pltpu.PrefetchScalarGridSpec(
            num_scalar_prefetch=2,
            grid=(n_blocks // MOE_GROUP,),
            in_specs=[pl.BlockSpec((step_rows, LANES), lambda i, be, nb: (i, 0))] + w_specs,
            out_specs=pl.BlockSpec((step_rows, LANES), lambda i, be, nb: (i, 0)),
        ),
        out_shape=jax.ShapeDtypeStruct(xd_tiles.shape, F32),
        compiler_params=pltpu.CompilerParams(
            dimension_semantics=("arbitrary",), vmem_limit_bytes=VMEM_LIMIT),
        name="expert_mlp",
    )(block_e, n_used, xd_tiles, *w_args)


def _combine_kernel(dest_ref, next_dest_ref, x1_ref, slab_ref, g_ref, yd_ref, o_ref, ybuf, sems):
    step = pl.program_id(0)
    slot_rows = TOP_K * ROW_DMA_TOKENS

    def gather(idx_ref, slot):
        _issue_row_copies(
            lambda t, k: _row_tile_copy(yd_ref, idx_ref[0, 0, t * TOP_K + k], ybuf,
                                        slot * slot_rows + k * ROW_DMA_TOKENS + t,
                                        sems.at[slot]))

    @pl.when(step == 0)
    def _():
        gather(dest_ref, 0)

    @pl.when(step + 1 < pl.num_programs(0))
    def _():
        gather(next_dest_ref, (step + 1) % 2)

    slot = step % 2
    first = slot * slot_rows
    pltpu.make_async_copy(
        yd_ref.at[pl.ds(0, slot_rows * ROW_TILE), :],
        ybuf.at[pl.ds(pl.multiple_of(first * ROW_TILE, ROW_TILE), slot_rows * ROW_TILE), :],
        sems.at[slot]).wait()

    slab = slab_ref[...]
    w0 = slab[:, SLAB_W0:SLAB_W0 + 1]
    w1 = slab[:, SLAB_W1:SLAB_W1 + 1]
    x = (x1_ref[...] + w0 * _load_row_tiles(ybuf, ROW_DMA_TOKENS, first_row=first)
         + w1 * _load_row_tiles(ybuf, ROW_DMA_TOKENS, first_row=first + ROW_DMA_TOKENS))
    ms = jnp.mean(x * x, axis=-1, keepdims=True)
    o_ref[...] = x * lax.rsqrt(ms + EPS) * g_ref[...]


def _combine(dest3, x1, slab, g_final, yd_tiles):
    n = x1.shape[0]
    steps = n // ROW_DMA_TOKENS
    return pl.pallas_call(
        _combine_kernel,
        grid=(steps,),
        in_specs=[
            pl.BlockSpec((1, 1, TOP_K * ROW_DMA_TOKENS), lambda i: (i, 0, 0),
                         memory_space=pltpu.SMEM),
            pl.BlockSpec((1, 1, TOP_K * ROW_DMA_TOKENS),
                         lambda i: (jnp.minimum(i + 1, steps - 1), 0, 0),
                         memory_space=pltpu.SMEM),
            pl.BlockSpec((ROW_DMA_TOKENS, D_MODEL), lambda i: (i, 0)),
            pl.BlockSpec((ROW_DMA_TOKENS, LANES), lambda i: (i, 0)),
            pl.BlockSpec((1, D_MODEL), lambda i: (0, 0)),
            pl.BlockSpec(memory_space=pl.ANY),
        ],
        out_specs=pl.BlockSpec((ROW_DMA_TOKENS, D_MODEL), lambda i: (i, 0)),
        out_shape=jax.ShapeDtypeStruct((n, D_MODEL), F32),
        scratch_shapes=[pltpu.VMEM((2 * TOP_K * ROW_DMA_TOKENS * ROW_TILE, LANES), F32),
                        pltpu.SemaphoreType.DMA((2,))],
        compiler_params=pltpu.CompilerParams(
            dimension_semantics=("arbitrary",), vmem_limit_bytes=VMEM_LIMIT),
        name="combine_norm",
    )(dest3, dest3, x1, slab, g_final, yd_tiles)


def _rope_tables(seq):
    pos = jnp.arange(seq, dtype=F32)
    inv_freq = 1.0 / (ROPE_THETA ** (jnp.arange(0, DIFF_QK_DIM, 2, dtype=F32) / DIFF_QK_DIM))
    freqs = pos[:, None] * inv_freq[None, :]
    emb = jnp.concatenate([freqs, freqs], axis=-1)
    cos, sin = jnp.cos(emb), jnp.sin(emb)
    half = DIFF_QK_DIM // 2
    sin_signed = jnp.concatenate([-sin[:, :half], sin[:, half:]], axis=-1)
    reps = LANES // DIFF_QK_DIM
    return (jnp.tile(cos, (1, reps)), jnp.tile(sin_signed, (1, reps)), cos.T, sin_signed.T)


def _split_in_projection(w):
    edges = [0]
    for width in (SB_WIDTH, SB_WIDTH, SB_WIDTH, DIFF_QK_WIDTH, DIFF_QK_WIDTH, DIFF_V_WIDTH,
                  D_MODEL, D_MODEL):
        edges.append(edges[-1] + width)
    sb_q, sb_k, sb_v, d_q, d_k, d_v, gate_a, gate_b = (
        w[:, lo:hi] for lo, hi in zip(edges[:-1], edges[1:]))
    w_rows = jnp.concatenate([sb_k, d_k, gate_a, gate_b], axis=1)
    w_feat = jnp.concatenate([d_q, d_v], axis=1)
    w_feat_sb = jnp.concatenate([sb_q, sb_v], axis=1)
    return w_rows.astype(BF16), w_feat.astype(BF16), w_feat_sb.astype(BF16)


def _split_bf16(w):
    hi = w.astype(BF16)
    lo = (w - hi.astype(F32)).astype(BF16)
    return hi, lo


def kernel(x, g_norm_mix, w_in, lambda_q1, lambda_k1, lambda_q2, lambda_k2, g_subln,
           w_up_a, w_up_b, w_out, g_norm_ffn, w_router_group, b_router_group,
           w_router_expert, b_router_expert, w_expert_gate, w_expert_up, w_expert_down,
           g_norm_final):
    batch, seq, d = x.shape
    depth = w_in.shape[0]
    n_tok = batch * seq
    assert depth == 1, "the final RMSNorm is fused into the layer's combine step"
    assert d == D_MODEL and seq % DIFF_BLOCK == 0 and seq % PROJ_ROWS == 0
    assert seq % SB_BLOCK == 0 and PROJ_ROWS % SB_BLOCK == 0
    assert DIFF_BLOCK == PROJ_ROWS, "feature-major q/v blocks are written per projection step"
    assert (N_EXPERTS * MOE_BLOCK) % (n_tok // ROW_DMA_TOKENS) == 0
    assert n_tok % POST_ROWS == 0 and n_tok % ROW_DMA_TOKENS == 0
    n_assign = n_tok * TOP_K
    n_blocks = n_assign // MOE_BLOCK + N_EXPERTS
    n_rows = n_blocks * MOE_BLOCK

    rope = _rope_tables(seq)
    x2 = x.reshape(n_tok, d)
    for l in range(depth):
        lambda_init = 0.8 - 0.6 * math.exp(-0.3 * l)
        w_rows, w_feat, w_feat_sb = _split_in_projection(w_in[l])
        proj, proj_t, proj_ts = _in_projection(x2, g_norm_mix[l][None, :], w_rows, w_feat,
                                               w_feat_sb, rope, seq)
        o_a = _sb_attention(proj, proj_ts, batch, seq)
        lam_rows = jnp.stack([lambda_q1[l], lambda_k1[l], lambda_q2[l], lambda_k2[l]])
        o_b = _diff_attention(proj, proj_t, lam_rows, g_subln[l][:, None], batch, seq,
                              lambda_init)

        w_r = jnp.zeros((d, LANES), F32)
        w_r = w_r.at[:, :N_GROUPS].set(w_router_group[l])
        w_r = w_r.at[:, N_GROUPS:N_GROUPS + N_EXPERTS].set(w_router_expert[l])
        b_r = jnp.zeros((1, LANES), F32)
        b_r = b_r.at[0, :N_GROUPS].set(b_router_group[l])
        b_r = b_r.at[0, N_GROUPS:N_GROUPS + N_EXPERTS].set(b_router_expert[l])
        wr_hi_lo = jnp.concatenate(_split_bf16(w_r), axis=1)

        x1, h2, slab, cnt = _post_attention(
            o_a, o_b, proj, x2, w_up_a[l].astype(BF16), w_up_b[l].astype(BF16),
            w_out[l].astype(BF16), g_norm_ffn[l][None, :], wr_hi_lo, b_r)

        counts = cnt[0, N_GROUPS:N_GROUPS + N_EXPERTS].astype(jnp.int32)
        padded = (counts + MOE_BLOCK - 1) // MOE_BLOCK * MOE_BLOCK
        pad_end = jnp.cumsum(padded)
        pad_start = pad_end - padded
        e_ids = slab[:, SLAB_E0:SLAB_E1 + 1].astype(jnp.int32)
        ranks = slab[:, SLAB_R0:SLAB_R1 + 1].astype(jnp.int32)
        expert_iota = jnp.arange(N_EXPERTS, dtype=jnp.int32)
        dest = ranks + jnp.sum(
            jnp.where(e_ids[:, :, None] == expert_iota, pad_start, 0), axis=-1)
        steps = n_tok // ROW_DMA_TOKENS
        dest3 = dest.reshape(steps, 1, TOP_K * ROW_DMA_TOKENS)
        gap = padded - counts
        gap_end = jnp.cumsum(gap)
        z = jnp.arange(n_rows - n_assign, dtype=jnp.int32)
        seg = jnp.sum((gap_end[None, :] <= z[:, None]).astype(jnp.int32), axis=1)
        seg_onehot = seg[:, None] == jnp.arange(N_EXPERTS + 1, dtype=jnp.int32)
        seg_first_row = jnp.concatenate([pad_start + counts, pad_end[-1:]])
        seg_first_z = jnp.concatenate([gap_end - gap, gap_end[-1:]])
        pad_rows = z + jnp.sum(jnp.where(seg_onehot, seg_first_row - seg_first_z, 0), axis=1)
        pad3 = pad_rows.reshape(steps, 1, (n_rows - n_assign) // steps)
        block_start = jnp.arange(n_blocks, dtype=jnp.int32) * MOE_BLOCK
        block_e = jnp.minimum(
            jnp.sum((pad_end[None, :] <= block_start[:, None]).astype(jnp.int32), axis=1),
            N_EXPERTS - 1)
        n_used = (pad_end[-1:] // MOE_BLOCK).astype(jnp.int32)

        xd = _scatter_rows(dest3, pad3, h2, n_rows)
        yd = _expert_mlp(block_e, n_used, xd, w_expert_gate[l].astype(BF16),
                         w_expert_up[l].astype(BF16), w_expert_down[l].astype(BF16))
        x2 = _combine(dest3, x1, slab, g_norm_final[None, :], yd)
    return x2.reshape(batch, seq, d)
```
